```python
import math
import jax
import jax.numpy as jnp
from jax import lax
import numpy as np


D_MODEL = 2048
BATCH = 16
SEQ = 2048
DEPTH = 2

D_MIX = D_MODEL
GDN_HEADS = 8
GDN_HEAD_DIM = 128
GDN_WIDTH = GDN_HEADS * GDN_HEAD_DIM
GDN_CONV = 4
GDN_CHUNK = 64
HGRN_HEADS = 4
HGRN_EXPAND = 128
HGRN_HEAD_DIM = 128
HGRN_KEY_WIDTH = HGRN_HEADS * HGRN_EXPAND
HGRN_WIDTH = HGRN_HEADS * HGRN_HEAD_DIM
HGRN_CHUNK = 16
SC_GROUPS = 4
SC_WIDTH = D_MIX - GDN_WIDTH - HGRN_WIDTH
SC_CONV = 3
D_IN_PROJ = 4 * GDN_WIDTH + 2 * GDN_HEADS + 2 * HGRN_KEY_WIDTH + 2 * HGRN_WIDTH + 3 * SC_WIDTH
N_GROUPS = 8
EXPERTS_PER_GROUP = 8
N_EXPERTS = N_GROUPS * EXPERTS_PER_GROUP
D_FF = 512
TOP_K = 2
MOE_BLOCK = 256
EPS = 1e-6
F32 = jnp.float32

kernel_name = 'hybrid_gdn_hgrn2_shortconv_hmoe'


def rms_norm(x, w):
    xf = x.astype(F32)
    y = xf * lax.rsqrt(jnp.mean(xf * xf, axis=-1, keepdims=True) + EPS)
    return (y * w.astype(F32)).astype(x.dtype)


def l2norm(x):
    return x * lax.rsqrt(jnp.sum(x * x, axis=-1, keepdims=True) + EPS)


def causal_depthwise_conv(x, w):
    k = w.shape[0]
    return lax.conv_general_dilated(
        x, w[:, None, :].astype(x.dtype), window_strides=(1,), padding=[(k - 1, 0)],
        dimension_numbers=('NWC', 'WIO', 'NWC'), feature_group_count=x.shape[-1])


def to_chunks(x, c):
    b, t, h, d = x.shape
    return x.reshape(b, t // c, c, h, d).transpose(0, 3, 1, 2, 4)


def split_columns(proj):
    sizes = (3 * GDN_WIDTH, GDN_WIDTH, GDN_HEADS, GDN_HEADS,
             HGRN_KEY_WIDTH, HGRN_KEY_WIDTH, HGRN_WIDTH, HGRN_WIDTH,
             SC_WIDTH, SC_WIDTH, SC_WIDTH)
    out, off = [], 0
    for s in sizes:
        out.append(proj[..., off:off + s])
        off += s
    return out


def gated_delta_net(qkv, z, b_pre, a_pre, conv_w, a_log, dt_bias, norm_w):
    bsz, t, _ = qkv.shape
    out_dtype = qkv.dtype
    qkv = jax.nn.silu(causal_depthwise_conv(qkv, conv_w)).astype(F32)
    q, k, v = jnp.split(qkv, 3, axis=-1)
    shp = (bsz, t, GDN_HEADS, GDN_HEAD_DIM)
    q = l2norm(q.reshape(shp)) * (GDN_HEAD_DIM ** -0.5)
    k = l2norm(k.reshape(shp))
    v = v.reshape(shp)
    c = GDN_CHUNK
    n = t // c
    beta = jax.nn.sigmoid(b_pre.astype(F32))
    g = -jnp.exp(a_log.astype(F32)) * jax.nn.softplus(a_pre.astype(F32) + dt_bias.astype(F32))
    beta = beta.reshape(bsz, n, c, GDN_HEADS).transpose(0, 3, 1, 2)
    g = g.reshape(bsz, n, c, GDN_HEADS).transpose(0, 3, 1, 2)
    q, k, v = to_chunks(q, c), to_chunks(k, c), to_chunks(v, c)
    g_cum = jnp.cumsum(g, axis=-1)
    causal = jnp.tril(jnp.ones((c, c), dtype=bool))
    strict = jnp.tril(jnp.ones((c, c), dtype=bool), -1)
    decay = jnp.exp(jnp.where(causal, g_cum[..., :, None] - g_cum[..., None, :], -jnp.inf))
    k_beta = k * beta[..., None]
    lower = jnp.where(strict, jnp.einsum('bhnid,bhnjd->bhnij', k_beta, k) * decay, 0.0)
    rhs = jnp.concatenate([v * beta[..., None], k_beta * jnp.exp(g_cum)[..., None]], axis=-1)
    sol = lax.linalg.triangular_solve(lower + jnp.eye(c, dtype=F32), rhs, left_side=True, lower=True)
    u, w = sol[..., :GDN_HEAD_DIM], sol[..., GDN_HEAD_DIM:]
    attn = jnp.einsum('bhnid,bhnjd->bhnij', q, k) * decay
    g_last = g_cum[..., -1:]
    q_dec = q * jnp.exp(g_cum)[..., None]
    k_dec = k * jnp.exp(g_last - g_cum)[..., None]
    tot = jnp.exp(g_last)

    def step(s, xs):
        q_c, k_c, u_c, w_c, a_c, t_c = xs
        v_new = u_c - jnp.einsum('bhid,bhdv->bhiv', w_c, s)
        o = jnp.einsum('bhid,bhdv->bhiv', q_c, s) + jnp.einsum('bhij,bhjv->bhiv', a_c, v_new)
        s = s * t_c[..., None] + jnp.einsum('bhjd,bhjv->bhdv', k_c, v_new)
        return s, o

    xs = tuple(jnp.moveaxis(a, 2, 0) for a in (q_dec, k_dec, u, w, attn, tot))
    s0 = jnp.zeros((bsz, GDN_HEADS, GDN_HEAD_DIM, GDN_HEAD_DIM), F32)
    _, o = lax.scan(step, s0, xs)
    o = o.transpose(1, 0, 3, 2, 4).reshape(shp)
    o = rms_norm(o, norm_w) * jax.nn.silu(z.astype(F32).reshape(shp))
    return o.reshape(bsz, t, GDN_WIDTH).astype(out_dtype)


def hgrn2(q, f_pre, i, gate, lb, norm_w):
    bsz, t, _ = q.shape
    out_dtype = q.dtype
    lb = lb.astype(F32)
    f_pre = f_pre.astype(F32)
    log_f = jnp.logaddexp(jnp.log(lb), jnp.log1p(-lb) + jax.nn.log_sigmoid(f_pre))
    k = (1.0 - lb) * jax.nn.sigmoid(-f_pre)
    shp_k = (bsz, t, HGRN_HEADS, HGRN_EXPAND)
    shp_v = (bsz, t, HGRN_HEADS, HGRN_HEAD_DIM)
    c = HGRN_CHUNK
    q_c = to_chunks(q.astype(F32).reshape(shp_k), c)
    k_c = to_chunks(k.reshape(shp_k), c)
    v_c = to_chunks(i.astype(F32).reshape(shp_v), c)
    b = jnp.cumsum(to_chunks(log_f.reshape(shp_k), c), axis=3)
    b_last = b[..., -1:, :]
    q_dec = q_c * jnp.exp(b)
    k_dec = k_c * jnp.exp(b_last - b)
    causal = jnp.tril(jnp.ones((c, c), dtype=bool))[:, :, None]

    def step(s, xs):
        q_t, k_t, v_t, b_t, qd, kd, bl = xs
        decay = jnp.exp(jnp.where(causal, b_t[..., :, None, :] - b_t[..., None, :, :], -jnp.inf))
        attn = jnp.einsum('bhid,bhjd,bhijd->bhij', q_t, k_t, decay)
        o = jnp.einsum('bhid,bhdv->bhiv', qd, s) + jnp.einsum('bhij,bhjv->bhiv', attn, v_t)
        s = s * jnp.swapaxes(jnp.exp(bl), -1, -2) + jnp.einsum('bhjd,bhjv->bhdv', kd, v_t)
        return s, o

    xs = tuple(jnp.moveaxis(a, 2, 0) for a in (q_c, k_c, v_c, b, q_dec, k_dec, b_last))
    s0 = jnp.zeros((bsz, HGRN_HEADS, HGRN_EXPAND, HGRN_HEAD_DIM), F32)
    _, o = lax.scan(step, s0, xs)
    o = o.transpose(1, 0, 3, 2, 4).reshape(shp_v)
    o = rms_norm(o, norm_w) * jax.nn.silu(gate.astype(F32).reshape(shp_v))
    return o.reshape(bsz, t, HGRN_WIDTH).astype(out_dtype)


def short_conv_mixer(b_gate, c_gate, h, conv_w):
    return b_gate * causal_depthwise_conv(c_gate * h, conv_w)


def hierarchical_moe(h, w_rg, b_rg, w_re, b_re, w_gate, w_up, w_down):
    bsz, t, d = h.shape
    xt = h.reshape(-1, d)
    n_tok = xt.shape[0]
    g_prob = jax.nn.softmax((xt @ w_rg).astype(F32) + b_rg.astype(F32), axis=-1)
    p_group, group = lax.top_k(g_prob, 1)
    e_logits = ((xt @ w_re).astype(F32) + b_re.astype(F32)).reshape(n_tok, N_GROUPS, EXPERTS_PER_GROUP)
    e_logits = jnp.take_along_axis(e_logits, group[:, :, None], axis=1)[:, 0]
    p_exp, local = lax.top_k(jax.nn.softmax(e_logits, axis=-1), TOP_K)
    weights = p_group * p_exp / jnp.sum(p_exp, axis=-1, keepdims=True)
    expert = group * EXPERTS_PER_GROUP + local
    flat_e = expert.reshape(-1).astype(jnp.int32)
    flat_w = weights.reshape(-1)
    flat_tok = jnp.repeat(jnp.arange(n_tok, dtype=jnp.int32), TOP_K)
    n_assign = n_tok * TOP_K
    order = jnp.argsort(flat_e)
    se, stok, sw = flat_e[order], flat_tok[order], flat_w[order]
    counts = jnp.bincount(flat_e, length=N_EXPERTS).astype(jnp.int32)
    starts = jnp.cumsum(counts) - counts
    padded = (counts + MOE_BLOCK - 1) // MOE_BLOCK * MOE_BLOCK
    pad_end = jnp.cumsum(padded)
    pad_start = pad_end - padded
    dest = pad_start[se] + jnp.arange(n_assign, dtype=jnp.int32) - starts[se]
    n_blocks = -(-n_assign // MOE_BLOCK) + N_EXPERTS
    row_tok = jnp.zeros((n_blocks * MOE_BLOCK,), jnp.int32).at[dest].set(stok)
    row_w = jnp.zeros((n_blocks * MOE_BLOCK,), F32).at[dest].set(sw)
    block_start = jnp.arange(n_blocks, dtype=jnp.int32) * MOE_BLOCK
    block_expert = jnp.minimum(jnp.searchsorted(pad_end, block_start, side='right'), N_EXPERTS - 1)

    def step(acc, xs):
        tok, wt, e = xs
        xb = xt[tok]
        hid = jax.nn.silu(xb @ w_gate[e]) * (xb @ w_up[e])
        y = (hid @ w_down[e]).astype(F32) * wt[:, None]
        return acc.at[tok].add(y), None

    acc, _ = lax.scan(step, jnp.zeros((n_tok, d), F32),
                      (row_tok.reshape(n_blocks, MOE_BLOCK), row_w.reshape(n_blocks, MOE_BLOCK), block_expert))
    return acc.reshape(bsz, t, d).astype(h.dtype)


def setup_inputs(seed: int = 0) -> dict:
    key = jax.random.key(seed)
    ks = jax.random.split(key, 22)

    def nrm(k, shape, scale):
        return jax.random.normal(k, shape, F32) * scale

    x = nrm(ks[0], (BATCH, SEQ, D_MODEL), 1.0)
    attn_norm_w = 1.0 + nrm(ks[1], (DEPTH, D_MODEL), 0.02)
    w_in = nrm(ks[2], (DEPTH, D_MODEL, D_IN_PROJ), D_MODEL ** -0.5)
    gdn_conv_w = nrm(ks[3], (DEPTH, GDN_CONV, 3 * GDN_WIDTH), GDN_CONV ** -0.5)
    gdn_a_log = jnp.log(jax.random.uniform(ks[4], (DEPTH, GDN_HEADS), F32, 1.0, 16.0))
    dt = jnp.exp(jax.random.uniform(ks[5], (DEPTH, GDN_HEADS), F32, math.log(1e-3), math.log(1e-1)))
    gdn_dt_bias = dt + jnp.log(-jnp.expm1(-dt))
    gdn_norm_w = 1.0 + nrm(ks[6], (DEPTH, GDN_HEAD_DIM), 0.02)
    hgrn_lower_bounds = 1.0 + nrm(ks[7], (DEPTH, HGRN_KEY_WIDTH), 0.1)
    hgrn_norm_w = 1.0 + nrm(ks[8], (DEPTH, HGRN_HEAD_DIM), 0.02)
    sc_conv_w = nrm(ks[9], (DEPTH, SC_CONV, SC_WIDTH), SC_CONV ** -0.5)
    w_out = nrm(ks[10], (DEPTH, D_MIX, D_MODEL), D_MIX ** -0.5)
    ffn_norm_w = 1.0 + nrm(ks[11], (DEPTH, D_MODEL), 0.02)
    w_router_group = nrm(ks[12], (DEPTH, D_MODEL, N_GROUPS), D_MODEL ** -0.5)
    b_router_group = nrm(ks[13], (DEPTH, N_GROUPS), 0.01)
    w_router_expert = nrm(ks[14], (DEPTH, D_MODEL, N_EXPERTS), D_MODEL ** -0.5)
    b_router_expert = nrm(ks[15], (DEPTH, N_EXPERTS), 0.01)
    w_gate = nrm(ks[16], (DEPTH, N_EXPERTS, D_MODEL, D_FF), D_MODEL ** -0.5)
    w_up = nrm(ks[17], (DEPTH, N_EXPERTS, D_MODEL, D_FF), D_MODEL ** -0.5)
    w_down = nrm(ks[18], (DEPTH, N_EXPERTS, D_FF, D_MODEL), D_FF ** -0.5)
    final_norm_w = 1.0 + nrm(ks[19], (D_MODEL,), 0.02)
    return {'x': x, 'attn_norm_w': attn_norm_w, 'w_in': w_in, 'gdn_conv_w': gdn_conv_w,
            'gdn_a_log': gdn_a_log, 'gdn_dt_bias': gdn_dt_bias, 'gdn_norm_w': gdn_norm_w,
            'hgrn_lower_bounds': hgrn_lower_bounds, 'hgrn_norm_w': hgrn_norm_w,
            'sc_conv_w': sc_conv_w, 'w_out': w_out, 'ffn_norm_w': ffn_norm_w,
            'w_router_group': w_router_group, 'b_router_group': b_router_group,
            'w_router_expert': w_router_expert, 'b_router_expert': b_router_expert,
            'w_gate': w_gate, 'w_up': w_up, 'w_down': w_down, 'final_norm_w': final_norm_w}


def reference(x, attn_norm_w, w_in, gdn_conv_w, gdn_a_log, gdn_dt_bias, gdn_norm_w,
              hgrn_lower_bounds, hgrn_norm_w, sc_conv_w, w_out, ffn_norm_w,
              w_router_group, b_router_group, w_router_expert, b_router_expert,
              w_gate, w_up, w_down, final_norm_w):
    lb_all = jnp.cumsum(jax.nn.softmax(hgrn_lower_bounds.astype(F32), axis=0), axis=0)
    lb_all = lb_all - lb_all[0]
    for l in range(DEPTH):
        h = rms_norm(x, attn_norm_w[l])
        proj = h @ w_in[l]
        (qkv, z, gb, ga, hq, hf, hi, hg, sb, sc, sh) = split_columns(proj)
        y_a = gated_delta_net(qkv, z, gb, ga, gdn_conv_w[l], gdn_a_log[l], gdn_dt_bias[l], gdn_norm_w[l])
        y_b = hgrn2(hq, hf, hi, hg, lb_all[l], hgrn_norm_w[l])
        y_c = short_conv_mixer(sb, sc, sh, sc_conv_w[l])
        mix = jnp.concatenate([y_a, y_b, y_c.astype(y_a.dtype)], axis=-1)
        x = x + (mix @ w_out[l]).astype(x.dtype)
        h = rms_norm(x, ffn_norm_w[l])
        x = x + hierarchical_moe(h, w_router_group[l], b_router_group[l], w_router_expert[l],
                                 b_router_expert[l], w_gate[l], w_up[l], w_down[l]).astype(x.dtype)
    return rms_norm(x, final_norm_w)
```

```python
import functools

import jax
import jax.numpy as jnp
from jax import lax
from jax.experimental import pallas as pl
from jax.experimental.pallas import tpu as pltpu

F32 = jnp.float32
BF16 = jnp.bfloat16
EPS = 1e-6

GDN_H, GDN_D, GDN_C, GDN_K = 8, 128, 64, 4
GDN_W = GDN_H * GDN_D
HG_H, HG_D, HG_C = 4, 128, 16
HG_W = HG_H * HG_D
SC_W, SC_K = 512, 3
N_GROUPS, EPG, N_EXP, D_FF, TOP_K = 8, 8, 64, 512, 2
MOE_BLK = 256
N_GATE_COLS = 2 * GDN_H

LANES = 128
SUBLANES = 8
VMEM_LIMIT = 56 * 1024 * 1024

PROJ_W = 4 * GDN_W + 2 * HG_W + 2 * HG_W + 3 * SC_W
OFF_QKV, OFF_Z = 0, 3 * GDN_W
OFF_HQ = 4 * GDN_W
OFF_HF, OFF_HI, OFF_HG = OFF_HQ + HG_W, OFF_HQ + 2 * HG_W, OFF_HQ + 3 * HG_W
OFF_SB = OFF_HQ + 4 * HG_W
OFF_SC, OFF_SH = OFF_SB + SC_W, OFF_SB + 2 * SC_W


def _dot(a, b):
    return jnp.dot(a, b, preferred_element_type=F32)


def _dot_nt(a, b):
    return lax.dot_general(a, b, (((1,), (1,)), ((), ())), preferred_element_type=F32)


def _dot_tn(a, b):
    return lax.dot_general(a, b, (((0,), (0,)), ((), ())), preferred_element_type=F32)


def _split_hi_lo(x):
    hi = x.astype(BF16)
    lo = (x - hi.astype(F32)).astype(BF16)
    return hi, lo


def _dot_exact_lhs(mask_bf16, x):
    hi, lo = _split_hi_lo(x)
    return _dot(mask_bf16, hi) + _dot(mask_bf16, lo)


def _sigmoid(x):
    return 1.0 / (1.0 + jnp.exp(-x))


def _silu(x):
    return x * _sigmoid(x)


def _softplus(x):
    return jnp.maximum(x, 0.0) + jnp.log1p(jnp.exp(-jnp.abs(x)))


def _rms_scale(x):
    return lax.rsqrt(jnp.mean(x * x, axis=-1, keepdims=True) + EPS)


def _combine_norm_kernel(*refs, n_add, write_x, final):
    it = iter(refs)
    x_ref = next(it)
    add_refs = [next(it) for _ in range(n_add)]
    nw_ref = next(it)
    wg_ref = None if final else next(it)
    xo_ref = next(it) if write_x else None
    xv = x_ref[...]
    for a in add_refs:
        xv = xv + a[...]
    if write_x:
        xo_ref[...] = xv
    hn = xv * _rms_scale(xv) * nw_ref[...]
    if final:
        next(it)[...] = hn
    else:
        h_ref, g_ref = next(it), next(it)
        hb = hn.astype(BF16)
        h_ref[...] = hb
        g_ref[...] = _dot(hb, wg_ref[...])


def _combine_norm(x, y2, norm_w, w_gates, *, final, tm=256):
    n, d = x.shape
    n_add = 0 if y2 is None else 2
    write_x = (n_add > 0) and not final
    row = lambda i: (i, 0)
    in_specs = [pl.BlockSpec((tm, d), row)]
    args = [x]
    if n_add:
        in_specs += [pl.BlockSpec((tm, d), row), pl.BlockSpec((tm, d), lambda i: (n // tm + i, 0))]
        args += [y2, y2]
    in_specs.append(pl.BlockSpec((1, d), lambda i: (0, 0)))
    args.append(norm_w.reshape(1, d))
    out_shape, out_specs = [], []
    if not final:
        in_specs.append(pl.BlockSpec((d, LANES), lambda i: (0, 0)))
        args.append(w_gates)
    if write_x:
        out_shape.append(jax.ShapeDtypeStruct((n, d), F32))
        out_specs.append(pl.BlockSpec((tm, d), row))
    if final:
        out_shape.append(jax.ShapeDtypeStruct((n, d), F32))
        out_specs.append(pl.BlockSpec((tm, d), row))
    else:
        out_shape += [jax.ShapeDtypeStruct((n, d), BF16), jax.ShapeDtypeStruct((n, LANES), F32)]
        out_specs += [pl.BlockSpec((tm, d), row), pl.BlockSpec((tm, LANES), row)]
    return pl.pallas_call(
        functools.partial(_combine_norm_kernel, n_add=n_add, write_x=write_x, final=final),
        grid=(n // tm,),
        in_specs=in_specs,
        out_specs=out_specs,
        out_shape=out_shape,
        compiler_params=pltpu.CompilerParams(dimension_semantics=("arbitrary",), vmem_limit_bytes=VMEM_LIMIT),
        name="combine_norm_final" if final else "combine_norm",
    )(*args)


def _inproj_kernel(h_ref, w_ref, o_ref):
    o_ref[...] = _dot(h_ref[...], w_ref[...]).astype(BF16)


def _inproj(h, w_main, *, tm=1024, tn=1280):
    n, d = h.shape
    pw = w_main.shape[1]
    return pl.pallas_call(
        _inproj_kernel,
        grid=(pw // tn, n // tm),
        in_specs=[pl.BlockSpec((tm, d), lambda j, i: (i, 0)), pl.BlockSpec((d, tn), lambda j, i: (0, j))],
        out_specs=pl.BlockSpec((tm, tn), lambda j, i: (i, j)),
        out_shape=jax.ShapeDtypeStruct((n, pw), BF16),
        compiler_params=pltpu.CompilerParams(dimension_semantics=("arbitrary", "arbitrary"), vmem_limit_bytes=VMEM_LIMIT),
        name="inproj",
    )(h, w_main)


def _unit_lower_inverse_minus_eye(L, eye, bd16, off32, off64):
    mm = lambda a, b: _dot(a.astype(BF16), b.astype(BF16))
    ld = jnp.where(bd16, L, 0.0)
    l2 = mm(ld, ld)
    l4 = mm(l2, l2)
    l8 = mm(l4, l4)
    t16 = mm(mm(eye - ld, eye + l2), mm(eye + l4, eye + l8))
    t32 = t16 - mm(mm(t16, jnp.where(off32, L, 0.0)), t16)
    t64 = t32 - mm(mm(t32, jnp.where(off64, L, 0.0)), t32)
    return t64 - eye


def _gdn_kernel(qkv_ref, z_ref, gat_ref, cw_ref, alog_ref, dtb_ref, nw_ref, o_ref,
                xs_ref, carry_ref, s_ref, *, tt):
    c, d, w = GDN_C, GDN_D, GDN_W

    @pl.when(pl.program_id(1) == 0)
    def _():
        carry_ref[...] = jnp.zeros_like(carry_ref)
        s_ref[...] = jnp.zeros_like(s_ref)

    x = qkv_ref[...].astype(F32)
    xp = jnp.concatenate([carry_ref[...], x], axis=0)
    base = SUBLANES - (GDN_K - 1)
    acc = cw_ref[0:1, :] * xp[base:base + tt]
    for k in range(1, GDN_K):
        acc = acc + cw_ref[k:k + 1, :] * xp[base + k:base + k + tt]
    carry_ref[...] = x[tt - SUBLANES:tt]
    xs_ref[...] = _silu(acc)

    row = lax.broadcasted_iota(jnp.int32, (c, c), 0)
    col = lax.broadcasted_iota(jnp.int32, (c, c), 1)
    causal = row >= col
    strict = row > col
    eye = (row == col).astype(F32)
    bd16 = (row // 16) == (col // 16)
    off32 = ((row // 32) == (col // 32)) & ((row // 16) != (col // 16))
    off64 = (row // 32) != (col // 32)
    tril = causal.astype(BF16)
    alog = alog_ref[...]
    dtb = dtb_ref[...]
    nw = nw_ref[...]

    def chunk(ci, carry):
        r0 = pl.multiple_of(ci * c, c)
        rows = pl.ds(r0, c)
        gat = gat_ref[rows, :]
        beta_all = _sigmoid(gat)
        g_all = -jnp.exp(alog) * _softplus(gat + dtb)
        gc_all = _dot_exact_lhs(tril, g_all)
        gc_t = jnp.concatenate([gc_all, jnp.zeros((LANES - c, LANES), F32)], axis=0).T
        for h in range(GDN_H):
            q = xs_ref[rows, h * d:(h + 1) * d]
            k = xs_ref[rows, w + h * d:w + (h + 1) * d]
            v = xs_ref[rows, 2 * w + h * d:2 * w + (h + 1) * d]
            qn = q * lax.rsqrt(jnp.sum(q * q, axis=-1, keepdims=True) + EPS) * (d ** -0.5)
            kn = k * lax.rsqrt(jnp.sum(k * k, axis=-1, keepdims=True) + EPS)
            beta = beta_all[:, h:h + 1]
            gc = gc_all[:, GDN_H + h:GDN_H + h + 1]
            gl = gc[c - 1:c, :]
            g_row = gc_t[GDN_H + h:GDN_H + h + 1, 0:c]
            decay = jnp.exp(jnp.where(causal, gc - g_row, -jnp.inf))
            egc = jnp.exp(gc)
            kb = kn * beta
            knb = kn.astype(BF16)
            kk = _dot_nt(kb.astype(BF16), knb)
            qk = _dot_nt(qn.astype(BF16), knb)
            lower = jnp.where(strict, kk * decay, 0.0)
            attn = qk * decay
            tm1 = _unit_lower_inverse_minus_eye(lower, eye, bd16, off32, off64)
            rhs = jnp.concatenate([v * beta, kb * egc], axis=1)
            sol = rhs + _dot(tm1.astype(BF16), rhs.astype(BF16))
            u, wy = sol[:, :d], sol[:, d:]
            s = s_ref[h]
            sb = s.astype(BF16)
            v_new = u - _dot(wy.astype(BF16), sb)
            vnb = v_new.astype(BF16)
            o = _dot((qn * egc).astype(BF16), sb) + _dot(attn.astype(BF16), vnb)
            kd = kn * jnp.exp(gl - gc)
            s_ref[h] = s * jnp.exp(gl) + _dot_tn(kd.astype(BF16), vnb)
            zz = z_ref[rows, h * d:(h + 1) * d].astype(F32)
            o_ref[rows, h * d:(h + 1) * d] = (o * _rms_scale(o) * nw * _silu(zz)).astype(BF16)
        return carry

    lax.fori_loop(0, tt // c, chunk, 0)


def _gdn(proj, gates, conv_w, a_log, dt_bias, norm_w, *, bsz, t, tt=256):
    n = proj.shape[0]
    nt = t // tt
    pad = LANES - N_GATE_COLS
    alog = jnp.pad(a_log.astype(F32), (GDN_H, pad)).reshape(1, LANES)
    dtb = jnp.pad(dt_bias.astype(F32), (GDN_H, pad)).reshape(1, LANES)
    rowmap = lambda b, i: (b * nt + i, 0)
    const = lambda b, i: (0, 0)
    return pl.pallas_call(
        functools.partial(_gdn_kernel, tt=tt),
        grid=(bsz, nt),
        in_specs=[
            pl.BlockSpec((tt, 3 * GDN_W), rowmap),
            pl.BlockSpec((tt, GDN_W), lambda b, i: (b * nt + i, OFF_Z // GDN_W)),
            pl.BlockSpec((tt, LANES), rowmap),
            pl.BlockSpec((GDN_K, 3 * GDN_W), const),
            pl.BlockSpec((1, LANES), const),
            pl.BlockSpec((1, LANES), const),
            pl.BlockSpec((1, GDN_D), const),
        ],
        out_specs=pl.BlockSpec((tt, GDN_W), rowmap),
        out_shape=jax.ShapeDtypeStruct((n, GDN_W), BF16),
        scratch_shapes=[
            pltpu.VMEM((tt, 3 * GDN_W), F32),
            pltpu.VMEM((SUBLANES, 3 * GDN_W), F32),
            pltpu.VMEM((GDN_H, GDN_D, GDN_D), F32),
        ],
        compiler_params=pltpu.CompilerParams(dimension_semantics=("arbitrary", "arbitrary"), vmem_limit_bytes=VMEM_LIMIT),
        name="gdn",
    )(proj, proj, gates, conv_w.astype(F32), alog, dtb, norm_w.astype(F32).reshape(1, GDN_D))


def _hgrn_sc_kernel(hq_ref, hf_ref, hi_ref, hg_ref, sb_ref, sc_ref, sh_ref, lbraw_ref, nw_ref, scw_ref,
                    o_ref, qs_ref, ks_ref, vs_ref, bs_ref, st_ref, carry_ref, *, tt, layer):
    c, d = HG_C, HG_D

    @pl.when(pl.program_id(1) == 0)
    def _():
        carry_ref[...] = jnp.zeros_like(carry_ref)
        st_ref[...] = jnp.zeros_like(st_ref)

    lbraw = lbraw_ref[...]
    e = jnp.exp(lbraw - jnp.max(lbraw, axis=0, keepdims=True))
    p = e / jnp.sum(e, axis=0, keepdims=True)
    lb = jnp.zeros((1, HG_W), F32)
    for kk in range(1, layer + 1):
        lb = lb + p[kk:kk + 1, :]

    fpre = hf_ref[...].astype(F32)
    log_sig = jnp.minimum(fpre, 0.0) - jnp.log1p(jnp.exp(-jnp.abs(fpre)))
    a = jnp.log(lb)
    cc = jnp.log1p(-lb) + log_sig
    log_f = jnp.maximum(a, cc) + jnp.log1p(jnp.exp(-jnp.abs(a - cc)))
    ri = lax.broadcasted_iota(jnp.int32, (tt, tt), 0)
    rj = lax.broadcasted_iota(jnp.int32, (tt, tt), 1)
    blocktril = (((ri // c) == (rj // c)) & (ri >= rj)).astype(BF16)
    bs_ref[...] = _dot_exact_lhs(blocktril, log_f)
    ks_ref[...] = (1.0 - lb) * _sigmoid(-fpre)
    qs_ref[...] = hq_ref[...].astype(F32)
    vs_ref[...] = hi_ref[...].astype(F32)

    prod = sc_ref[...].astype(F32) * sh_ref[...].astype(F32)
    xp = jnp.concatenate([carry_ref[...], prod], axis=0)
    base = SUBLANES - (SC_K - 1)
    conv = scw_ref[0:1, :] * xp[base:base + tt]
    for kk in range(1, SC_K):
        conv = conv + scw_ref[kk:kk + 1, :] * xp[base + kk:base + kk + tt]
    carry_ref[...] = prod[tt - SUBLANES:tt]
    o_ref[:, HG_W:HG_W + SC_W] = (sb_ref[...].astype(F32) * conv).astype(BF16)

    rowi = lax.broadcasted_iota(jnp.int32, (c, d), 0)
    ones = jnp.ones((d, d), BF16)
    nw = nw_ref[...]

    def chunk(ci, carry):
        r0 = pl.multiple_of(ci * c, c)
        rows = pl.ds(r0, c)
        for h in range(HG_H):
            cs = slice(h * d, (h + 1) * d)
            q, k, v, b = qs_ref[rows, cs], ks_ref[rows, cs], vs_ref[rows, cs], bs_ref[rows, cs]
            bl = b[c - 1:c, :]
            st = st_ref[h]
            o_inter = _dot_nt((q * jnp.exp(b)).astype(BF16), st.astype(BF16))
            parts = []
            for j in range(c):
                pj = q * k[j:j + 1, :] * jnp.exp(jnp.minimum(b - b[j:j + 1, :], 0.0))
                parts.append(jnp.where(rowi >= j, pj, 0.0))
            pmat = jnp.concatenate(parts, axis=0)
            rsum = _dot(pmat.astype(BF16), ones)
            o_intra = rsum[0:c] * v[0:1, :]
            for j in range(1, c):
                o_intra = o_intra + rsum[j * c:(j + 1) * c] * v[j:j + 1, :]
            o = o_inter + o_intra
            kd = k * jnp.exp(bl - b)
            st_ref[h] = st * jnp.exp(bl) + _dot_tn(v.astype(BF16), kd.astype(BF16))
            gate = hg_ref[rows, cs].astype(F32)
            o_ref[rows, cs] = (o * _rms_scale(o) * nw * _silu(gate)).astype(BF16)
        return carry

    lax.fori_loop(0, tt // c, chunk, 0)


def _hgrn_sc(proj, lb_raw, norm_w, sc_conv_w, *, bsz, t, layer, tt=256):
    n = proj.shape[0]
    nt = t // tt
    depth = lb_raw.shape[0]
    slab = lambda off: pl.BlockSpec((tt, HG_W), lambda b, i: (b * nt + i, off // HG_W))
    const = lambda b, i: (0, 0)
    return pl.pallas_call(
        functools.partial(_hgrn_sc_kernel, tt=tt, layer=layer),
        grid=(bsz, nt),
        in_specs=[slab(OFF_HQ), slab(OFF_HF), slab(OFF_HI), slab(OFF_HG), slab(OFF_SB), slab(OFF_SC), slab(OFF_SH),
                  pl.BlockSpec((depth, HG_W), const),
                  pl.BlockSpec((1, HG_D), const),
                  pl.BlockSpec((SC_K, SC_W), const)],
        out_specs=pl.BlockSpec((tt, HG_W + SC_W), lambda b, i: (b * nt + i, 0)),
        out_shape=jax.ShapeDtypeStruct((n, HG_W + SC_W), BF16),
        scratch_shapes=[pltpu.VMEM((tt, HG_W), F32)] * 4 + [
            pltpu.VMEM((HG_H, HG_D, HG_D), F32),
            pltpu.VMEM((SUBLANES, SC_W), F32),
        ],
        compiler_params=pltpu.CompilerParams(dimension_semantics=("arbitrary", "arbitrary"), vmem_limit_bytes=VMEM_LIMIT),
        name="hgrn_sc",
    )(proj, proj, proj, proj, proj, proj, proj, lb_raw.astype(F32), norm_w.astype(F32).reshape(1, HG_D),
      sc_conv_w.astype(F32))


def _outproj_router_kernel(ya_ref, ybc_ref, x_ref, wo_ref, nw_ref, wrh_ref, wrl_ref, br_ref,
                           x2_ref, eid_ref, wts_ref):
    tm = x_ref.shape[0]
    x2 = x_ref[...] + _dot(ya_ref[...], wo_ref[0:GDN_W, :]) + _dot(ybc_ref[...], wo_ref[GDN_W:, :])
    x2_ref[...] = x2
    h = x2 * _rms_scale(x2) * nw_ref[...]
    hh, hl = _split_hi_lo(h)
    wrh, wrl = wrh_ref[...], wrl_ref[...]
    logits = _dot(hh, wrh) + (_dot(hh, wrl) + _dot(hl, wrh)) + br_ref[...]

    lane_i = lax.broadcasted_iota(jnp.int32, (tm, LANES), 1)
    lane = lane_i.astype(F32)
    lane_grp = ((lane_i - N_GROUPS) // EPG).astype(F32)
    is_g = lane_i < N_GROUPS
    is_e = (lane_i >= N_GROUPS) & (lane_i < N_GROUPS + N_EXP)
    big = float(4 * LANES)

    lg = jnp.where(is_g, logits, -jnp.inf)
    mg = jnp.max(lg, axis=-1, keepdims=True)
    p_group = 1.0 / jnp.sum(jnp.exp(lg - mg), axis=-1, keepdims=True)
    gidx = jnp.min(jnp.where(lg == mg, lane, big), axis=-1, keepdims=True)

    le = jnp.where(is_e & (lane_grp == gidx), logits, -jnp.inf)
    m1 = jnp.max(le, axis=-1, keepdims=True)
    i1 = jnp.min(jnp.where(le == m1, lane, big), axis=-1, keepdims=True)
    le2 = jnp.where(lane == i1, -jnp.inf, le)
    m2 = jnp.max(le2, axis=-1, keepdims=True)
    i2 = jnp.min(jnp.where(le2 == m2, lane, big), axis=-1, keepdims=True)
    e2 = jnp.exp(m2 - m1)
    w1 = p_group / (1.0 + e2)
    w2 = p_group * e2 / (1.0 + e2)
    eid = jnp.where(lane_i == 0, i1 - N_GROUPS, jnp.where(lane_i == 1, i2 - N_GROUPS, 0.0))
    eid_ref[...] = eid.astype(jnp.int32)
    wts_ref[...] = jnp.where(lane_i == 0, w1, jnp.where(lane_i == 1, w2, 0.0))


def _outproj_router(ya, ybc, x, w_out, norm_w, wr_hi, wr_lo, b_r, *, tm=256):
    n, d = x.shape
    row = lambda i: (i, 0)
    const = lambda i: (0, 0)
    return pl.pallas_call(
        _outproj_router_kernel,
        grid=(n // tm,),
        in_specs=[pl.BlockSpec((tm, GDN_W), row), pl.BlockSpec((tm, HG_W + SC_W), row), pl.BlockSpec((tm, d), row),
                  pl.BlockSpec((d, d), const), pl.BlockSpec((1, d), const),
                  pl.BlockSpec((d, LANES), const), pl.BlockSpec((d, LANES), const), pl.BlockSpec((1, LANES), const)],
        out_specs=[pl.BlockSpec((tm, d), row), pl.BlockSpec((tm, LANES), row), pl.BlockSpec((tm, LANES), row)],
        out_shape=[jax.ShapeDtypeStruct((n, d), F32), jax.ShapeDtypeStruct((n, LANES), jnp.int32),
                   jax.ShapeDtypeStruct((n, LANES), F32)],
        compiler_params=pltpu.CompilerParams(dimension_semantics=("arbitrary",), vmem_limit_bytes=VMEM_LIMIT),
        name="outproj_router",
    )(ya, ybc, x, w_out, norm_w.reshape(1, d), wr_hi, wr_lo, b_r)


def _moe_plan(eid, wts, n_tok):
    blk = MOE_BLK
    n_assign = n_tok * TOP_K
    n_blocks = -(-n_assign // blk) + N_EXP
    n_rows = n_blocks * blk
    flat_e = eid.reshape(-1)
    flat_w = wts.reshape(-1)
    order = jnp.argsort(flat_e, stable=True).astype(jnp.int32)
    counts = jnp.sum((flat_e[:, None] == jnp.arange(N_EXP, dtype=jnp.int32)[None, :]).astype(jnp.int32), axis=0)
    starts = jnp.cumsum(counts) - counts
    padded = (counts + blk - 1) // blk * blk
    pad_end = jnp.cumsum(padded)
    pad_start = pad_end - padded
    r = jnp.arange(n_rows, dtype=jnp.int32)
    e_r = jnp.minimum(jnp.searchsorted(pad_end, r, side="right"), N_EXP - 1).astype(jnp.int32)
    j = r - pad_start[e_r]
    valid = (j < counts[e_r]) & (r < pad_end[-1])
    src = jnp.clip(starts[e_r] + j, 0, n_assign - 1)
    a = order[src]
    tok = a // TOP_K
    slot = a % TOP_K
    row_tok = jnp.where(valid, tok, 0).astype(jnp.int32)
    row_dst = jnp.where(valid, slot * n_tok + tok, TOP_K * n_tok + r % blk).astype(jnp.int32)
    row_w = jnp.where(valid, flat_w[a], 0.0).astype(F32)
    n_active = (pad_end[-1] // blk).astype(jnp.int32)
    bidx = jnp.minimum(jnp.arange(n_blocks, dtype=jnp.int32), jnp.maximum(n_active - 1, 0))
    block_expert = e_r[bidx * blk]
    return (block_expert, n_active.reshape(1), row_tok.reshape(n_blocks, 1, blk), row_dst.reshape(n_blocks, 1, blk),
            row_w.reshape(n_blocks, blk, 1))


def _moe_kernel(bexp_ref, nact_ref, x_hbm, tok_ref, dst_ref, rw_ref, nw_ref, wg_ref, wu_ref, wd_ref, y_hbm,
                xbuf, ybuf, gsem, ssem):
    blk = MOE_BLK

    def gather_copy(r):
        return pltpu.make_async_copy(x_hbm.at[pl.ds(tok_ref[0, r], 1), :], xbuf.at[pl.ds(r, 1), :], gsem)

    def scatter_copy(r):
        return pltpu.make_async_copy(ybuf.at[pl.ds(r, 1), :], y_hbm.at[pl.ds(dst_ref[0, r], 1), :], ssem)

    @pl.when(pl.program_id(0) < nact_ref[0])
    def _():
        for r in range(blk):
            gather_copy(r).start()
        for r in range(blk):
            gather_copy(r).wait()
        xv = xbuf[...]
        hb = (xv * _rms_scale(xv) * nw_ref[...]).astype(BF16)
        g = _dot(hb, wg_ref[...].astype(BF16))
        u = _dot(hb, wu_ref[...].astype(BF16))
        hid = (_silu(g) * u).astype(BF16)
        ybuf[...] = _dot(hid, wd_ref[...].astype(BF16)) * rw_ref[...]
        for r in range(blk):
            scatter_copy(r).start()
        for r in range(blk):
            scatter_copy(r).wait()


def _moe_ffn(x2, plan, norm_w, w_gate, w_up, w_down, *, layer):
    n, d = x2.shape
    blk = MOE_BLK
    block_expert, n_active, row_tok, row_dst, row_w = plan
    n_blocks = row_tok.shape[0]
    wmap = lambda b, bexp, nact: (layer, bexp[b], 0, 0)
    grid_spec = pltpu.PrefetchScalarGridSpec(
        num_scalar_prefetch=2,
        grid=(n_blocks,),
        in_specs=[
            pl.BlockSpec(memory_space=pl.ANY),
            pl.BlockSpec((None, 1, blk), lambda b, bexp, nact: (b, 0, 0), memory_space=pltpu.SMEM),
            pl.BlockSpec((None, 1, blk), lambda b, bexp, nact: (b, 0, 0), memory_space=pltpu.SMEM),
            pl.BlockSpec((None, blk, 1), lambda b, bexp, nact: (b, 0, 0)),
            pl.BlockSpec((1, d), lambda b, bexp, nact: (0, 0)),
            pl.BlockSpec((None, None, d, D_FF), wmap),
            pl.BlockSpec((None, None, d, D_FF), wmap),
            pl.BlockSpec((None, None, D_FF, d), wmap),
        ],
        out_specs=pl.BlockSpec(memory_space=pl.ANY),
        scratch_shapes=[pltpu.VMEM((blk, d), F32), pltpu.VMEM((blk, d), F32),
                        pltpu.SemaphoreType.DMA, pltpu.SemaphoreType.DMA],
    )
    return pl.pallas_call(
        _moe_kernel,
        grid_spec=grid_spec,
        out_shape=jax.ShapeDtypeStruct((TOP_K * n + blk, d), F32),
        compiler_params=pltpu.CompilerParams(dimension_semantics=("arbitrary",), vmem_limit_bytes=VMEM_LIMIT),
        name="moe_ffn",
    )(block_expert, n_active, x2, row_tok, row_dst, row_w, norm_w.reshape(1, d), w_gate, w_up, w_down)


def _split_w_in(w_in_l):
    g0 = 4 * GDN_W
    w_main = jnp.concatenate([w_in_l[:, :g0], w_in_l[:, g0 + N_GATE_COLS:]], axis=1).astype(BF16)
    w_gates = jnp.pad(w_in_l[:, g0:g0 + N_GATE_COLS], ((0, 0), (0, LANES - N_GATE_COLS))).astype(BF16)
    return w_main, w_gates


def _router_weights(w_rg, b_rg, w_re, b_re):
    d = w_rg.shape[0]
    pad = LANES - N_GROUPS - N_EXP
    w_r = jnp.concatenate([w_rg, w_re, jnp.zeros((d, pad), F32)], axis=1).astype(F32)
    b_r = jnp.concatenate([b_rg, b_re, jnp.zeros((pad,), F32)]).astype(F32).reshape(1, LANES)
    hi, lo = _split_hi_lo(w_r)
    return hi, lo, b_r


def kernel(x, attn_norm_w, w_in, gdn_conv_w, gdn_a_log, gdn_dt_bias, gdn_norm_w, hgrn_lower_bounds, hgrn_norm_w, sc_conv_w, w_out, ffn_norm_w, w_router_group, b_router_group, w_router_expert, b_router_expert, w_gate, w_up, w_down, final_norm_w):
    bsz, t, d = x.shape
    n = bsz * t
    depth = w_in.shape[0]
    xs = x.reshape(n, d).astype(F32)
    y2 = None
    for l in range(depth):
        w_main, w_gates = _split_w_in(w_in[l])
        if y2 is None:
            h, gates = _combine_norm(xs, None, attn_norm_w[l], w_gates, final=False)
        else:
            xs, h, gates = _combine_norm(xs, y2, attn_norm_w[l], w_gates, final=False)
        proj = _inproj(h, w_main)
        ya = _gdn(proj, gates, gdn_conv_w[l], gdn_a_log[l], gdn_dt_bias[l], gdn_norm_w[l], bsz=bsz, t=t)
        ybc = _hgrn_sc(proj, hgrn_lower_bounds, hgrn_norm_w[l], sc_conv_w[l], bsz=bsz, t=t, layer=l)
        wr_hi, wr_lo, b_r = _router_weights(w_router_group[l], b_router_group[l], w_router_expert[l], b_router_expert[l])
        xs, eid, wts = _outproj_router(ya, ybc, xs, w_out[l].astype(BF16), ffn_norm_w[l], wr_hi, wr_lo, b_r)
        plan = _moe_plan(eid[:, :TOP_K], wts[:, :TOP_K], n)
        y2 = _moe_ffn(xs, plan, ffn_norm_w[l], w_gate, w_up, w_down, layer=l)
    (out,) = _combine_norm(xs, y2, final_norm_w, None, final=True)
    return out.reshape(bsz, t, d).astype(x.dtype)
```

```python
import functools

import jax
import jax.numpy as jnp
from jax import lax
from jax.experimental import pallas as pl
from jax.experimental.pallas import tpu as pltpu

F32 = jnp.float32
BF16 = jnp.bfloat16
EPS = 1e-6

GDN_H, GDN_D, GDN_C, GDN_K = 8, 128, 64, 4
GDN_W = GDN_H * GDN_D
HG_H, HG_D, HG_C = 4, 128, 16
HG_W = HG_H * HG_D
SC_W, SC_K = 512, 3
N_GROUPS, EPG, N_EXP, D_FF, TOP_K = 8, 8, 64, 512, 2
MOE_BLK = 256
N_GATE_COLS = 2 * GDN_H

LANES = 128
SUBLANES = 8
VMEM_LIMIT = 56 * 1024 * 1024

PROJ_W = 4 * GDN_W + 2 * HG_W + 2 * HG_W + 3 * SC_W
OFF_QKV, OFF_Z = 0, 3 * GDN_W
OFF_HQ = 4 * GDN_W
OFF_HF, OFF_HI, OFF_HG = OFF_HQ + HG_W, OFF_HQ + 2 * HG_W, OFF_HQ + 3 * HG_W
OFF_SB = OFF_HQ + 4 * HG_W
OFF_SC, OFF_SH = OFF_SB + SC_W, OFF_SB + 2 * SC_W


def _dot(a, b):
    return jnp.dot(a, b, preferred_element_type=F32)


def _dot_nt(a, b):
    return lax.dot_general(a, b, (((1,), (1,)), ((), ())), preferred_element_type=F32)


def _dot_tn(a, b):
    return lax.dot_general(a, b, (((0,), (0,)), ((), ())), preferred_element_type=F32)


def _split_hi_lo(x):
    hi = x.astype(BF16)
    lo = (x - hi.astype(F32)).astype(BF16)
    return hi, lo


def _dot_exact_lhs(mask_bf16, x):
    hi, lo = _split_hi_lo(x)
    return _dot(mask_bf16, hi) + _dot(mask_bf16, lo)


def _sigmoid(x):
    return 1.0 / (1.0 + jnp.exp(-x))


def _silu(x):
    return x * _sigmoid(x)


def _softplus(x):
    return jnp.maximum(x, 0.0) + jnp.log1p(jnp.exp(-jnp.abs(x)))


def _rms_scale(x):
    return lax.rsqrt(jnp.mean(x * x, axis=-1, keepdims=True) + EPS)


def _combine_norm_kernel(*refs, n_add, write_x, final):
    it = iter(refs)
    x_ref = next(it)
    add_refs = [next(it) for _ in range(n_add)]
    nw_ref = next(it)
    wg_ref = None if final else next(it)
    xo_ref = next(it) if write_x else None
    xv = x_ref[...]
    for a in add_refs:
        xv = xv + a[...]
    if write_x:
        xo_ref[...] = xv
    hn = xv * _rms_scale(xv) * nw_ref[...]
    if final:
        next(it)[...] = hn
    else:
        h_ref, g_ref = next(it), next(it)
        hb = hn.astype(BF16)
        h_ref[...] = hb
        g_ref[...] = _dot(hb, wg_ref[...])


def _combine_norm(x, y2, norm_w, w_gates, *, final, tm=256):
    n, d = x.shape
    n_add = 0 if y2 is None else 2
    write_x = (n_add > 0) and not final
    row = lambda i: (i, 0)
    in_specs = [pl.BlockSpec((tm, d), row)]
    args = [x]
    if n_add:
        in_specs += [pl.BlockSpec((tm, d), row), pl.BlockSpec((tm, d), lambda i: (n // tm + i, 0))]
        args += [y2, y2]
    in_specs.append(pl.BlockSpec((1, d), lambda i: (0, 0)))
    args.append(norm_w.reshape(1, d))
    out_shape, out_specs = [], []
    if not final:
        in_specs.append(pl.BlockSpec((d, LANES), lambda i: (0, 0)))
        args.append(w_gates)
    if write_x:
        out_shape.append(jax.ShapeDtypeStruct((n, d), F32))
        out_specs.append(pl.BlockSpec((tm, d), row))
    if final:
        out_shape.append(jax.ShapeDtypeStruct((n, d), F32))
        out_specs.append(pl.BlockSpec((tm, d), row))
    else:
        out_shape += [jax.ShapeDtypeStruct((n, d), BF16), jax.ShapeDtypeStruct((n, LANES), F32)]
        out_specs += [pl.BlockSpec((tm, d), row), pl.BlockSpec((tm, LANES), row)]
    return pl.pallas_call(
        functools.partial(_combine_norm_kernel, n_add=n_add, write_x=write_x, final=final),
        grid=(n // tm,),
        in_specs=in_specs,
        out_specs=out_specs,
        out_shape=out_shape,
        compiler_params=pltpu.CompilerParams(dimension_semantics=("arbitrary",), vmem_limit_bytes=VMEM_LIMIT),
        name="combine_norm_final" if final else "combine_norm",
    )(*args)


def _inproj_kernel(h_ref, w_ref, o_ref):
    o_ref[...] = _dot(h_ref[...], w_ref[...]).astype(BF16)


def _inproj(h, w_main, *, tm=1024, tn=1280):
    n, d = h.shape
    pw = w_main.shape[1]
    return pl.pallas_call(
        _inproj_kernel,
        grid=(pw // tn, n // tm),
        in_specs=[pl.BlockSpec((tm, d), lambda j, i: (i, 0)), pl.BlockSpec((d, tn), lambda j, i: (0, j))],
        out_specs=pl.BlockSpec((tm, tn), lambda j, i: (i, j)),
        out_shape=jax.ShapeDtypeStruct((n, pw), BF16),
        compiler_params=pltpu.CompilerParams(dimension_semantics=("arbitrary", "arbitrary"), vmem_limit_bytes=VMEM_LIMIT),
        name="inproj",
    )(h, w_main)


def _unit_lower_inverse_minus_eye(lowers, eye, bd16, off32, off64):
    mm = lambda a, b: _dot(a.astype(BF16), b.astype(BF16))
    ld = [jnp.where(bd16, m, 0.0) for m in lowers]
    l2 = [mm(a, a) for a in ld]
    l4 = [mm(a, a) for a in l2]
    l8 = [mm(a, a) for a in l4]
    pa = [mm(eye - a, eye + b) for a, b in zip(ld, l2)]
    pb = [mm(eye + a, eye + b) for a, b in zip(l4, l8)]
    t16 = [mm(a, b) for a, b in zip(pa, pb)]
    x1 = [mm(t, jnp.where(off32, m, 0.0)) for t, m in zip(t16, lowers)]
    t32 = [t - mm(x, t) for t, x in zip(t16, x1)]
    x2 = [mm(t, jnp.where(off64, m, 0.0)) for t, m in zip(t32, lowers)]
    return [t - mm(x, t) - eye for t, x in zip(t32, x2)]


def _gdn_kernel(qkv_ref, z_ref, gat_ref, cw_ref, alog_ref, dtb_ref, nw_ref, o_ref,
                xs_ref, carry_ref, s_ref, *, tt):
    c, d, w = GDN_C, GDN_D, GDN_W

    @pl.when(pl.program_id(1) == 0)
    def _():
        carry_ref[...] = jnp.zeros_like(carry_ref)
        s_ref[...] = jnp.zeros_like(s_ref)

    x = qkv_ref[...].astype(F32)
    xp = jnp.concatenate([carry_ref[...], x], axis=0)
    base = SUBLANES - (GDN_K - 1)
    acc = cw_ref[0:1, :] * xp[base:base + tt]
    for k in range(1, GDN_K):
        acc = acc + cw_ref[k:k + 1, :] * xp[base + k:base + k + tt]
    carry_ref[...] = x[tt - SUBLANES:tt]
    xs_ref[...] = _silu(acc)

    row = lax.broadcasted_iota(jnp.int32, (c, c), 0)
    col = lax.broadcasted_iota(jnp.int32, (c, c), 1)
    causal = row >= col
    strict = row > col
    eye = (row == col).astype(F32)
    bd16 = (row // 16) == (col // 16)
    off32 = ((row // 32) == (col // 32)) & ((row // 16) != (col // 16))
    off64 = (row // 32) != (col // 32)
    tril = causal.astype(BF16)
    alog = alog_ref[...]
    dtb = dtb_ref[...]
    nw = nw_ref[...]

    def chunk(ci, carry):
        r0 = pl.multiple_of(ci * c, c)
        rows = pl.ds(r0, c)
        gat = gat_ref[rows, :]
        beta_all = _sigmoid(gat)
        g_all = -jnp.exp(alog) * _softplus(gat + dtb)
        gc_all = _dot_exact_lhs(tril, g_all)
        gc_t = jnp.concatenate([gc_all, jnp.zeros((LANES - c, LANES), F32)], axis=0).T
        heads = range(GDN_H)
        q = [xs_ref[rows, h * d:(h + 1) * d] for h in heads]
        k = [xs_ref[rows, w + h * d:w + (h + 1) * d] for h in heads]
        v = [xs_ref[rows, 2 * w + h * d:2 * w + (h + 1) * d] for h in heads]
        qn = [a * lax.rsqrt(jnp.sum(a * a, axis=-1, keepdims=True) + EPS) * (d ** -0.5) for a in q]
        kn = [a * lax.rsqrt(jnp.sum(a * a, axis=-1, keepdims=True) + EPS) for a in k]
        beta = [beta_all[:, h:h + 1] for h in heads]
        gc = [gc_all[:, GDN_H + h:GDN_H + h + 1] for h in heads]
        gl = [a[c - 1:c, :] for a in gc]
        decay = [jnp.exp(jnp.where(causal, gc[h] - gc_t[GDN_H + h:GDN_H + h + 1, 0:c], -jnp.inf)) for h in heads]
        egc = [jnp.exp(a) for a in gc]
        kb = [a * b for a, b in zip(kn, beta)]
        knb = [a.astype(BF16) for a in kn]
        kk = [_dot_nt(a.astype(BF16), b) for a, b in zip(kb, knb)]
        qk = [_dot_nt(a.astype(BF16), b) for a, b in zip(qn, knb)]
        lower = [jnp.where(strict, a * b, 0.0) for a, b in zip(kk, decay)]
        attn = [(a * b).astype(BF16) for a, b in zip(qk, decay)]
        tm1 = _unit_lower_inverse_minus_eye(lower, eye, bd16, off32, off64)
        rhs = [jnp.concatenate([v[h] * beta[h], kb[h] * egc[h]], axis=1) for h in heads]
        sol = [r + _dot(t.astype(BF16), r.astype(BF16)) for t, r in zip(tm1, rhs)]
        qd = [(qn[h] * egc[h]).astype(BF16) for h in heads]
        kd = [(kn[h] * jnp.exp(gl[h] - gc[h])).astype(BF16) for h in heads]
        s = [s_ref[h] for h in heads]
        sb = [a.astype(BF16) for a in s]
        v_new = [sol[h][:, :d] - _dot(sol[h][:, d:].astype(BF16), sb[h]) for h in heads]
        vnb = [a.astype(BF16) for a in v_new]
        o = [_dot(qd[h], sb[h]) + _dot(attn[h], vnb[h]) for h in heads]
        s_new = [s[h] * jnp.exp(gl[h]) + _dot_tn(kd[h], vnb[h]) for h in heads]
        for h in heads:
            s_ref[h] = s_new[h]
        for h in heads:
            zz = z_ref[rows, h * d:(h + 1) * d].astype(F32)
            o_ref[rows, h * d:(h + 1) * d] = (o[h] * _rms_scale(o[h]) * nw * _silu(zz)).astype(BF16)
        return carry

    lax.fori_loop(0, tt // c, chunk, 0)


def _gdn(proj, gates, conv_w, a_log, dt_bias, norm_w, *, bsz, t, tt=256):
    n = proj.shape[0]
    nt = t // tt
    pad = LANES - N_GATE_COLS
    alog = jnp.pad(a_log.astype(F32), (GDN_H, pad)).reshape(1, LANES)
    dtb = jnp.pad(dt_bias.astype(F32), (GDN_H, pad)).reshape(1, LANES)
    rowmap = lambda b, i: (b * nt + i, 0)
    const = lambda b, i: (0, 0)
    return pl.pallas_call(
        functools.partial(_gdn_kernel, tt=tt),
        grid=(bsz, nt),
        in_specs=[
            pl.BlockSpec((tt, 3 * GDN_W), rowmap),
            pl.BlockSpec((tt, GDN_W), lambda b, i: (b * nt + i, OFF_Z // GDN_W)),
            pl.BlockSpec((tt, LANES), rowmap),
            pl.BlockSpec((GDN_K, 3 * GDN_W), const),
            pl.BlockSpec((1, LANES), const),
            pl.BlockSpec((1, LANES), const),
            pl.BlockSpec((1, GDN_D), const),
        ],
        out_specs=pl.BlockSpec((tt, GDN_W), rowmap),
        out_shape=jax.ShapeDtypeStruct((n, GDN_W), BF16),
        scratch_shapes=[
            pltpu.VMEM((tt, 3 * GDN_W), F32),
            pltpu.VMEM((SUBLANES, 3 * GDN_W), F32),
            pltpu.VMEM((GDN_H, GDN_D, GDN_D), F32),
        ],
        compiler_params=pltpu.CompilerParams(dimension_semantics=("arbitrary", "arbitrary"), vmem_limit_bytes=VMEM_LIMIT),
        name="gdn",
    )(proj, proj, gates, conv_w.astype(F32), alog, dtb, norm_w.astype(F32).reshape(1, GDN_D))


def _hgrn_sc_kernel(hq_ref, hf_ref, hi_ref, hg_ref, sb_ref, sc_ref, sh_ref, lbraw_ref, nw_ref, scw_ref,
                    o_ref, qs_ref, ks_ref, vs_ref, bs_ref, st_ref, carry_ref, *, tt, layer):
    c, d = HG_C, HG_D

    @pl.when(pl.program_id(1) == 0)
    def _():
        carry_ref[...] = jnp.zeros_like(carry_ref)
        st_ref[...] = jnp.zeros_like(st_ref)

    lbraw = lbraw_ref[...]
    e = jnp.exp(lbraw - jnp.max(lbraw, axis=0, keepdims=True))
    p = e / jnp.sum(e, axis=0, keepdims=True)
    lb = jnp.zeros((1, HG_W), F32)
    for kk in range(1, layer + 1):
        lb = lb + p[kk:kk + 1, :]

    fpre = hf_ref[...].astype(F32)
    log_sig = jnp.minimum(fpre, 0.0) - jnp.log1p(jnp.exp(-jnp.abs(fpre)))
    a = jnp.log(lb)
    cc = jnp.log1p(-lb) + log_sig
    log_f = jnp.maximum(a, cc) + jnp.log1p(jnp.exp(-jnp.abs(a - cc)))
    ri = lax.broadcasted_iota(jnp.int32, (tt, tt), 0)
    rj = lax.broadcasted_iota(jnp.int32, (tt, tt), 1)
    blocktril = (((ri // c) == (rj // c)) & (ri >= rj)).astype(BF16)
    bs_ref[...] = _dot_exact_lhs(blocktril, log_f)
    ks_ref[...] = (1.0 - lb) * _sigmoid(-fpre)
    qs_ref[...] = hq_ref[...].astype(F32)
    vs_ref[...] = hi_ref[...].astype(F32)

    prod = sc_ref[...].astype(F32) * sh_ref[...].astype(F32)
    xp = jnp.concatenate([carry_ref[...], prod], axis=0)
    base = SUBLANES - (SC_K - 1)
    conv = scw_ref[0:1, :] * xp[base:base + tt]
    for kk in range(1, SC_K):
        conv = conv + scw_ref[kk:kk + 1, :] * xp[base + kk:base + kk + tt]
    carry_ref[...] = prod[tt - SUBLANES:tt]
    o_ref[:, HG_W:HG_W + SC_W] = (sb_ref[...].astype(F32) * conv).astype(BF16)

    rowi = lax.broadcasted_iota(jnp.int32, (c, d), 0)
    ones = jnp.ones((d, d), BF16)
    nw = nw_ref[...]

    def chunk(ci, carry):
        r0 = pl.multiple_of(ci * c, c)
        rows = pl.ds(r0, c)
        for h in range(HG_H):
            cs = slice(h * d, (h + 1) * d)
            q, k, v, b = qs_ref[rows, cs], ks_ref[rows, cs], vs_ref[rows, cs], bs_ref[rows, cs]
            bl = b[c - 1:c, :]
            st = st_ref[h]
            o_inter = _dot_nt((q * jnp.exp(b)).astype(BF16), st.astype(BF16))
            parts = []
            for j in range(c):
                pj = q * k[j:j + 1, :] * jnp.exp(jnp.minimum(b - b[j:j + 1, :], 0.0))
                parts.append(jnp.where(rowi >= j, pj, 0.0))
            pmat = jnp.concatenate(parts, axis=0)
            rsum = _dot(pmat.astype(BF16), ones)
            o_intra = rsum[0:c] * v[0:1, :]
            for j in range(1, c):
                o_intra = o_intra + rsum[j * c:(j + 1) * c] * v[j:j + 1, :]
            o = o_inter + o_intra
            kd = k * jnp.exp(bl - b)
            st_ref[h] = st * jnp.exp(bl) + _dot_tn(v.astype(BF16), kd.astype(BF16))
            gate = hg_ref[rows, cs].astype(F32)
            o_ref[rows, cs] = (o * _rms_scale(o) * nw * _silu(gate)).astype(BF16)
        return carry

    lax.fori_loop(0, tt // c, chunk, 0)


def _hgrn_sc(proj, lb_raw, norm_w, sc_conv_w, *, bsz, t, layer, tt=256):
    n = proj.shape[0]
    nt = t // tt
    depth = lb_raw.shape[0]
    slab = lambda off: pl.BlockSpec((tt, HG_W), lambda b, i: (b * nt + i, off // HG_W))
    const = lambda b, i: (0, 0)
    return pl.pallas_call(
        functools.partial(_hgrn_sc_kernel, tt=tt, layer=layer),
        grid=(bsz, nt),
        in_specs=[slab(OFF_HQ), slab(OFF_HF), slab(OFF_HI), slab(OFF_HG), slab(OFF_SB), slab(OFF_SC), slab(OFF_SH),
                  pl.BlockSpec((depth, HG_W), const),
                  pl.BlockSpec((1, HG_D), const),
                  pl.BlockSpec((SC_K, SC_W), const)],
        out_specs=pl.BlockSpec((tt, HG_W + SC_W), lambda b, i: (b * nt + i, 0)),
        out_shape=jax.ShapeDtypeStruct((n, HG_W + SC_W), BF16),
        scratch_shapes=[pltpu.VMEM((tt, HG_W), F32)] * 4 + [
            pltpu.VMEM((HG_H, HG_D, HG_D), F32),
            pltpu.VMEM((SUBLANES, SC_W), F32),
        ],
        compiler_params=pltpu.CompilerParams(dimension_semantics=("arbitrary", "arbitrary"), vmem_limit_bytes=VMEM_LIMIT),
        name="hgrn_sc",
    )(proj, proj, proj, proj, proj, proj, proj, lb_raw.astype(F32), norm_w.astype(F32).reshape(1, HG_D),
      sc_conv_w.astype(F32))


def _outproj_router_kernel(ya_ref, ybc_ref, x_ref, wo_ref, nw_ref, wrh_ref, wrl_ref, br_ref,
                           x2_ref, eid_ref, wts_ref):
    tm = x_ref.shape[0]
    x2 = x_ref[...] + _dot(ya_ref[...], wo_ref[0:GDN_W, :]) + _dot(ybc_ref[...], wo_ref[GDN_W:, :])
    x2_ref[...] = x2
    h = x2 * _rms_scale(x2) * nw_ref[...]
    hh, hl = _split_hi_lo(h)
    wrh, wrl = wrh_ref[...], wrl_ref[...]
    logits = _dot(hh, wrh) + (_dot(hh, wrl) + _dot(hl, wrh)) + br_ref[...]

    lane_i = lax.broadcasted_iota(jnp.int32, (tm, LANES), 1)
    lane = lane_i.astype(F32)
    lane_grp = ((lane_i - N_GROUPS) // EPG).astype(F32)
    is_g = lane_i < N_GROUPS
    is_e = (lane_i >= N_GROUPS) & (lane_i < N_GROUPS + N_EXP)
    big = float(4 * LANES)

    lg = jnp.where(is_g, logits, -jnp.inf)
    mg = jnp.max(lg, axis=-1, keepdims=True)
    p_group = 1.0 / jnp.sum(jnp.exp(lg - mg), axis=-1, keepdims=True)
    gidx = jnp.min(jnp.where(lg == mg, lane, big), axis=-1, keepdims=True)

    le = jnp.where(is_e & (lane_grp == gidx), logits, -jnp.inf)
    m1 = jnp.max(le, axis=-1, keepdims=True)
    i1 = jnp.min(jnp.where(le == m1, lane, big), axis=-1, keepdims=True)
    le2 = jnp.where(lane == i1, -jnp.inf, le)
    m2 = jnp.max(le2, axis=-1, keepdims=True)
    i2 = jnp.min(jnp.where(le2 == m2, lane, big), axis=-1, keepdims=True)
    e2 = jnp.exp(m2 - m1)
    w1 = p_group / (1.0 + e2)
    w2 = p_group * e2 / (1.0 + e2)
    eid = jnp.where(lane_i == 0, i1 - N_GROUPS, jnp.where(lane_i == 1, i2 - N_GROUPS, 0.0))
    eid_ref[...] = eid.astype(jnp.int32)
    wts_ref[...] = jnp.where(lane_i == 0, w1, jnp.where(lane_i == 1, w2, 0.0))


def _outproj_router(ya, ybc, x, w_out, norm_w, wr_hi, wr_lo, b_r, *, tm=256):
    n, d = x.shape
    row = lambda i: (i, 0)
    const = lambda i: (0, 0)
    return pl.pallas_call(
        _outproj_router_kernel,
        grid=(n // tm,),
        in_specs=[pl.BlockSpec((tm, GDN_W), row), pl.BlockSpec((tm, HG_W + SC_W), row), pl.BlockSpec((tm, d), row),
                  pl.BlockSpec((d, d), const), pl.BlockSpec((1, d), const),
                  pl.BlockSpec((d, LANES), const), pl.BlockSpec((d, LANES), const), pl.BlockSpec((1, LANES), const)],
        out_specs=[pl.BlockSpec((tm, d), row), pl.BlockSpec((tm, LANES), row), pl.BlockSpec((tm, LANES), row)],
        out_shape=[jax.ShapeDtypeStruct((n, d), F32), jax.ShapeDtypeStruct((n, LANES), jnp.int32),
                   jax.ShapeDtypeStruct((n, LANES), F32)],
        compiler_params=pltpu.CompilerParams(dimension_semantics=("arbitrary",), vmem_limit_bytes=VMEM_LIMIT),
        name="outproj_router",
    )(ya, ybc, x, w_out, norm_w.reshape(1, d), wr_hi, wr_lo, b_r)


def _moe_plan(eid, wts, n_tok):
    blk = MOE_BLK
    n_assign = n_tok * TOP_K
    n_blocks = -(-n_assign // blk) + N_EXP
    n_rows = n_blocks * blk
    flat_e = eid.reshape(-1).astype(jnp.int32)
    flat_w = wts.reshape(-1)
    shift = max(1, (n_assign - 1).bit_length())
    keys = lax.sort(flat_e * (1 << shift) + jnp.arange(n_assign, dtype=jnp.int32))
    order = keys & ((1 << shift) - 1)
    experts = jnp.arange(N_EXP, dtype=jnp.int32)
    counts = jnp.sum((flat_e[:, None] == experts[None, :]).astype(jnp.int32), axis=0)
    starts = jnp.cumsum(counts) - counts
    padded = (counts + blk - 1) // blk * blk
    pad_end = jnp.cumsum(padded)
    pad_start = pad_end - padded
    r = jnp.arange(n_rows, dtype=jnp.int32)
    e_r = jnp.minimum(jnp.sum((r[:, None] >= pad_end[None, :]).astype(jnp.int32), axis=1), N_EXP - 1)
    j = r - pad_start[e_r]
    valid = (j < counts[e_r]) & (r < pad_end[-1])
    src = jnp.clip(starts[e_r] + j, 0, n_assign - 1)
    a = order[src]
    tok = a // TOP_K
    slot = a % TOP_K
    row_tok = jnp.where(valid, tok, 0).astype(jnp.int32)
    row_dst = jnp.where(valid, slot * n_tok + tok, TOP_K * n_tok + r % blk).astype(jnp.int32)
    row_w = jnp.where(valid, flat_w[a], 0.0).astype(F32)
    n_active = (pad_end[-1] // blk).astype(jnp.int32)
    bidx = jnp.minimum(jnp.arange(n_blocks, dtype=jnp.int32), jnp.maximum(n_active - 1, 0))
    block_expert = e_r[bidx * blk]
    return (block_expert, n_active.reshape(1), row_tok.reshape(n_blocks, 1, blk), row_dst.reshape(n_blocks, 1, blk),
            row_w.reshape(n_blocks, blk, 1))


def _moe_kernel(bexp_ref, nact_ref, x_hbm, tok_ref, dst_ref, rw_ref, nw_ref, wg_ref, wu_ref, wd_ref, y_hbm,
                xbuf, ybuf, gsem, ssem):
    blk = MOE_BLK

    def gather_copy(r):
        return pltpu.make_async_copy(x_hbm.at[pl.ds(tok_ref[0, r], 1), :], xbuf.at[pl.ds(r, 1), :], gsem)

    def scatter_copy(r):
        return pltpu.make_async_copy(ybuf.at[pl.ds(r, 1), :], y_hbm.at[pl.ds(dst_ref[0, r], 1), :], ssem)

    @pl.when(pl.program_id(0) < nact_ref[0])
    def _():
        for r in range(blk):
            gather_copy(r).start()
        for r in range(blk):
            gather_copy(r).wait()
        xv = xbuf[...]
        hb = (xv * _rms_scale(xv) * nw_ref[...]).astype(BF16)
        g = _dot(hb, wg_ref[...].astype(BF16))
        u = _dot(hb, wu_ref[...].astype(BF16))
        hid = (_silu(g) * u).astype(BF16)
        ybuf[...] = _dot(hid, wd_ref[...].astype(BF16)) * rw_ref[...]
        for r in range(blk):
            scatter_copy(r).start()
        for r in range(blk):
            scatter_copy(r).wait()


def _moe_ffn(x2, plan, norm_w, w_gate, w_up, w_down, *, layer):
    n, d = x2.shape
    blk = MOE_BLK
    block_expert, n_active, row_tok, row_dst, row_w = plan
    n_blocks = row_tok.shape[0]
    wmap = lambda b, bexp, nact: (layer, bexp[b], 0, 0)
    grid_spec = pltpu.PrefetchScalarGridSpec(
        num_scalar_prefetch=2,
        grid=(n_blocks,),
        in_specs=[
            pl.BlockSpec(memory_space=pl.ANY),
            pl.BlockSpec((None, 1, blk), lambda b, bexp, nact: (b, 0, 0), memory_space=pltpu.SMEM),
            pl.BlockSpec((None, 1, blk), lambda b, bexp, nact: (b, 0, 0), memory_space=pltpu.SMEM),
            pl.BlockSpec((None, blk, 1), lambda b, bexp, nact: (b, 0, 0)),
            pl.BlockSpec((1, d), lambda b, bexp, nact: (0, 0)),
            pl.BlockSpec((None, None, d, D_FF), wmap),
            pl.BlockSpec((None, None, d, D_FF), wmap),
            pl.BlockSpec((None, None, D_FF, d), wmap),
        ],
        out_specs=pl.BlockSpec(memory_space=pl.ANY),
        scratch_shapes=[pltpu.VMEM((blk, d), F32), pltpu.VMEM((blk, d), F32),
                        pltpu.SemaphoreType.DMA, pltpu.SemaphoreType.DMA],
    )
    return pl.pallas_call(
        _moe_kernel,
        grid_spec=grid_spec,
        out_shape=jax.ShapeDtypeStruct((TOP_K * n + blk, d), F32),
        compiler_params=pltpu.CompilerParams(dimension_semantics=("arbitrary",), vmem_limit_bytes=VMEM_LIMIT),
        name="moe_ffn",
    )(block_expert, n_active, x2, row_tok, row_dst, row_w, norm_w.reshape(1, d), w_gate, w_up, w_down)


def _split_w_in(w_in_l):
    g0 = 4 * GDN_W
    w_main = jnp.concatenate([w_in_l[:, :g0], w_in_l[:, g0 + N_GATE_COLS:]], axis=1).astype(BF16)
    w_gates = jnp.pad(w_in_l[:, g0:g0 + N_GATE_COLS], ((0, 0), (0, LANES - N_GATE_COLS))).astype(BF16)
    return w_main, w_gates


def _router_weights(w_rg, b_rg, w_re, b_re):
    d = w_rg.shape[0]
    pad = LANES - N_GROUPS - N_EXP
    w_r = jnp.concatenate([w_rg, w_re, jnp.zeros((d, pad), F32)], axis=1).astype(F32)
    b_r = jnp.concatenate([b_rg, b_re, jnp.zeros((pad,), F32)]).astype(F32).reshape(1, LANES)
    hi, lo = _split_hi_lo(w_r)
    return hi, lo, b_r


def kernel(x, attn_norm_w, w_in, gdn_conv_w, gdn_a_log, gdn_dt_bias, gdn_norm_w, hgrn_lower_bounds, hgrn_norm_w, sc_conv_w, w_out, ffn_norm_w, w_router_group, b_router_group, w_router_expert, b_router_expert, w_gate, w_up, w_down, final_norm_w):
    bsz, t, d = x.shape
    n = bsz * t
    depth = w_in.shape[0]
    xs = x.reshape(n, d).astype(F32)
    y2 = None
    for l in range(depth):
        w_main, w_gates = _split_w_in(w_in[l])
        if y2 is None:
            h, gates = _combine_norm(xs, None, attn_norm_w[l], w_gates, final=False)
        else:
            xs, h, gates = _combine_norm(xs, y2, attn_norm_w[l], w_gates, final=False)
        proj = _inproj(h, w_main)
        ya = _gdn(proj, gates, gdn_conv_w[l], gdn_a_log[l], gdn_dt_bias[l], gdn_norm_w[l], bsz=bsz, t=t)
        ybc = _hgrn_sc(proj, hgrn_lower_bounds, hgrn_norm_w[l], sc_conv_w[l], bsz=bsz, t=t, layer=l)
        wr_hi, wr_lo, b_r = _router_weights(w_router_group[l], b_router_group[l], w_router_expert[l], b_router_expert[l])
        xs, eid, wts = _outproj_router(ya, ybc, xs, w_out[l].astype(BF16), ffn_norm_w[l], wr_hi, wr_lo, b_r)
        plan = _moe_plan(eid[:, :TOP_K], wts[:, :TOP_K], n)
        y2 = _moe_ffn(xs, plan, ffn_norm_w[l], w_gate, w_up, w_down, layer=l)
    (out,) = _combine_norm(xs, y2, final_norm_w, None, final=True)
    return out.reshape(bsz, t, d).astype(x.dtype)
```

```python
import functools

import jax
import jax.numpy as jnp
from jax import lax
from jax.experimental import pallas as pl
from jax.experimental.pallas import tpu as pltpu

F32 = jnp.float32
BF16 = jnp.bfloat16
EPS = 1e-6

GDN_H, GDN_D, GDN_C, GDN_K = 8, 128, 64, 4
GDN_W = GDN_H * GDN_D
HG_H, HG_D, HG_C = 4, 128, 16
HG_W = HG_H * HG_D
SC_W, SC_K = 512, 3
N_GROUPS, EPG, N_EXP, D_FF, TOP_K = 8, 8, 64, 512, 2
MOE_BLK = 256
N_GATE_COLS = 2 * GDN_H

LANES = 128
SUBLANES = 8
VMEM_LIMIT = 56 * 1024 * 1024

PROJ_W = 4 * GDN_W + 2 * HG_W + 2 * HG_W + 3 * SC_W
OFF_QKV, OFF_Z = 0, 3 * GDN_W
OFF_HQ = 4 * GDN_W
OFF_HF, OFF_HI, OFF_HG = OFF_HQ + HG_W, OFF_HQ + 2 * HG_W, OFF_HQ + 3 * HG_W
OFF_SB = OFF_HQ + 4 * HG_W
OFF_SC, OFF_SH = OFF_SB + SC_W, OFF_SB + 2 * SC_W


def _dot(a, b):
    return jnp.dot(a, b, preferred_element_type=F32)


def _dot_nt(a, b):
    return lax.dot_general(a, b, (((1,), (1,)), ((), ())), preferred_element_type=F32)


def _dot_tn(a, b):
    return lax.dot_general(a, b, (((0,), (0,)), ((), ())), preferred_element_type=F32)


def _split_hi_lo(x):
    hi = x.astype(BF16)
    lo = (x - hi.astype(F32)).astype(BF16)
    return hi, lo


def _dot_exact_lhs(mask_bf16, x):
    hi, lo = _split_hi_lo(x)
    return _dot(mask_bf16, hi) + _dot(mask_bf16, lo)


def _sigmoid(x):
    return 1.0 / (1.0 + jnp.exp(-x))


def _silu(x):
    return x * _sigmoid(x)


def _softplus(x):
    return jnp.maximum(x, 0.0) + jnp.log1p(jnp.exp(-jnp.abs(x)))


def _rms_scale(x):
    return lax.rsqrt(jnp.mean(x * x, axis=-1, keepdims=True) + EPS)


def _combine_norm_kernel(*refs, n_add, write_x, final):
    it = iter(refs)
    x_ref = next(it)
    add_refs = [next(it) for _ in range(n_add)]
    nw_ref = next(it)
    wg_ref = None if final else next(it)
    xo_ref = next(it) if write_x else None
    xv = x_ref[...]
    for a in add_refs:
        xv = xv + a[...]
    if write_x:
        xo_ref[...] = xv
    hn = xv * _rms_scale(xv) * nw_ref[...]
    if final:
        next(it)[...] = hn
    else:
        h_ref, g_ref = next(it), next(it)
        hb = hn.astype(BF16)
        h_ref[...] = hb
        g_ref[...] = _dot(hb, wg_ref[...])


def _combine_norm(x, y2, norm_w, w_gates, *, final, tm=256):
    n, d = x.shape
    n_add = 0 if y2 is None else 2
    write_x = (n_add > 0) and not final
    row = lambda i: (i, 0)
    in_specs = [pl.BlockSpec((tm, d), row)]
    args = [x]
    if n_add:
        in_specs += [pl.BlockSpec((tm, d), row), pl.BlockSpec((tm, d), lambda i: (n // tm + i, 0))]
        args += [y2, y2]
    in_specs.append(pl.BlockSpec((1, d), lambda i: (0, 0)))
    args.append(norm_w.reshape(1, d))
    out_shape, out_specs = [], []
    if not final:
        in_specs.append(pl.BlockSpec((d, LANES), lambda i: (0, 0)))
        args.append(w_gates)
    if write_x:
        out_shape.append(jax.ShapeDtypeStruct((n, d), F32))
        out_specs.append(pl.BlockSpec((tm, d), row))
    if final:
        out_shape.append(jax.ShapeDtypeStruct((n, d), F32))
        out_specs.append(pl.BlockSpec((tm, d), row))
    else:
        out_shape += [jax.ShapeDtypeStruct((n, d), BF16), jax.ShapeDtypeStruct((n, LANES), F32)]
        out_specs += [pl.BlockSpec((tm, d), row), pl.BlockSpec((tm, LANES), row)]
    return pl.pallas_call(
        functools.partial(_combine_norm_kernel, n_add=n_add, write_x=write_x, final=final),
        grid=(n // tm,),
        in_specs=in_specs,
        out_specs=out_specs,
        out_shape=out_shape,
        compiler_params=pltpu.CompilerParams(dimension_semantics=("arbitrary",), vmem_limit_bytes=VMEM_LIMIT),
        name="combine_norm_final" if final else "combine_norm",
    )(*args)


def _inproj_kernel(h_ref, w_ref, o_ref):
    o_ref[...] = _dot(h_ref[...], w_ref[...]).astype(BF16)


def _inproj(h, w_main, *, tm=1024, tn=1280):
    n, d = h.shape
    pw = w_main.shape[1]
    return pl.pallas_call(
        _inproj_kernel,
        grid=(pw // tn, n // tm),
        in_specs=[pl.BlockSpec((tm, d), lambda j, i: (i, 0)), pl.BlockSpec((d, tn), lambda j, i: (0, j))],
        out_specs=pl.BlockSpec((tm, tn), lambda j, i: (i, j)),
        out_shape=jax.ShapeDtypeStruct((n, pw), BF16),
        compiler_params=pltpu.CompilerParams(dimension_semantics=("arbitrary", "arbitrary"), vmem_limit_bytes=VMEM_LIMIT),
        name="inproj",
    )(h, w_main)


def _unit_lower_inverse_minus_eye(lowers, eye, bd16, off32, off64):
    mm = lambda a, b: _dot(a.astype(BF16), b.astype(BF16))
    ld = [jnp.where(bd16, m, 0.0) for m in lowers]
    l2 = [mm(a, a) for a in ld]
    l4 = [mm(a, a) for a in l2]
    l8 = [mm(a, a) for a in l4]
    pa = [mm(eye - a, eye + b) for a, b in zip(ld, l2)]
    pb = [mm(eye + a, eye + b) for a, b in zip(l4, l8)]
    t16 = [mm(a, b) for a, b in zip(pa, pb)]
    x1 = [mm(t, jnp.where(off32, m, 0.0)) for t, m in zip(t16, lowers)]
    t32 = [t - mm(x, t) for t, x in zip(t16, x1)]
    x2 = [mm(t, jnp.where(off64, m, 0.0)) for t, m in zip(t32, lowers)]
    return [t - mm(x, t) - eye for t, x in zip(t32, x2)]


def _gdn_kernel(qkv_ref, z_ref, gat_ref, cw_ref, alog_ref, dtb_ref, nw_ref, o_ref,
                xs_ref, carry_ref, s_ref, *, tt):
    c, d, w = GDN_C, GDN_D, GDN_W

    @pl.when(pl.program_id(1) == 0)
    def _():
        carry_ref[...] = jnp.zeros_like(carry_ref)
        s_ref[...] = jnp.zeros_like(s_ref)

    x = qkv_ref[...].astype(F32)
    xp = jnp.concatenate([carry_ref[...], x], axis=0)
    base = SUBLANES - (GDN_K - 1)
    acc = cw_ref[0:1, :] * xp[base:base + tt]
    for k in range(1, GDN_K):
        acc = acc + cw_ref[k:k + 1, :] * xp[base + k:base + k + tt]
    carry_ref[...] = x[tt - SUBLANES:tt]
    xs_ref[...] = _silu(acc)

    row = lax.broadcasted_iota(jnp.int32, (c, c), 0)
    col = lax.broadcasted_iota(jnp.int32, (c, c), 1)
    causal = row >= col
    strict = row > col
    eye = (row == col).astype(F32)
    bd16 = (row // 16) == (col // 16)
    off32 = ((row // 32) == (col // 32)) & ((row // 16) != (col // 16))
    off64 = (row // 32) != (col // 32)
    tril = causal.astype(BF16)
    alog = alog_ref[...]
    dtb = dtb_ref[...]
    nw = nw_ref[...]

    def chunk(ci, carry):
        r0 = pl.multiple_of(ci * c, c)
        rows = pl.ds(r0, c)
        gat = gat_ref[rows, :]
        beta_all = _sigmoid(gat)
        g_all = -jnp.exp(alog) * _softplus(gat + dtb)
        gc_all = _dot_exact_lhs(tril, g_all)
        gc_t = jnp.concatenate([gc_all, jnp.zeros((LANES - c, LANES), F32)], axis=0).T
        heads = range(GDN_H)
        q = [xs_ref[rows, h * d:(h + 1) * d] for h in heads]
        k = [xs_ref[rows, w + h * d:w + (h + 1) * d] for h in heads]
        v = [xs_ref[rows, 2 * w + h * d:2 * w + (h + 1) * d] for h in heads]
        qn = [a * lax.rsqrt(jnp.sum(a * a, axis=-1, keepdims=True) + EPS) * (d ** -0.5) for a in q]
        kn = [a * lax.rsqrt(jnp.sum(a * a, axis=-1, keepdims=True) + EPS) for a in k]
        beta = [beta_all[:, h:h + 1] for h in heads]
        gc = [gc_all[:, GDN_H + h:GDN_H + h + 1] for h in heads]
        gl = [a[c - 1:c, :] for a in gc]
        decay = [jnp.exp(jnp.where(causal, gc[h] - gc_t[GDN_H + h:GDN_H + h + 1, 0:c], -jnp.inf)) for h in heads]
        egc = [jnp.exp(a) for a in gc]
        kb = [a * b for a, b in zip(kn, beta)]
        knb = [a.astype(BF16) for a in kn]
        kk = [_dot_nt(a.astype(BF16), b) for a, b in zip(kb, knb)]
        qk = [_dot_nt(a.astype(BF16), b) for a, b in zip(qn, knb)]
        lower = [jnp.where(strict, a * b, 0.0) for a, b in zip(kk, decay)]
        attn = [(a * b).astype(BF16) for a, b in zip(qk, decay)]
        tm1 = _unit_lower_inverse_minus_eye(lower, eye, bd16, off32, off64)
        rhs = [jnp.concatenate([v[h] * beta[h], kb[h] * egc[h]], axis=1) for h in heads]
        sol = [r + _dot(t.astype(BF16), r.astype(BF16)) for t, r in zip(tm1, rhs)]
        qd = [(qn[h] * egc[h]).astype(BF16) for h in heads]
        kd = [(kn[h] * jnp.exp(gl[h] - gc[h])).astype(BF16) for h in heads]
        s = [s_ref[h] for h in heads]
        sb = [a.astype(BF16) for a in s]
        v_new = [sol[h][:, :d] - _dot(sol[h][:, d:].astype(BF16), sb[h]) for h in heads]
        vnb = [a.astype(BF16) for a in v_new]
        o = [_dot(qd[h], sb[h]) + _dot(attn[h], vnb[h]) for h in heads]
        s_new = [s[h] * jnp.exp(gl[h]) + _dot_tn(kd[h], vnb[h]) for h in heads]
        for h in heads:
            s_ref[h] = s_new[h]
        for h in heads:
            zz = z_ref[rows, h * d:(h + 1) * d].astype(F32)
            o_ref[rows, h * d:(h + 1) * d] = (o[h] * _rms_scale(o[h]) * nw * _silu(zz)).astype(BF16)
        return carry

    lax.fori_loop(0, tt // c, chunk, 0)


def _gdn(proj, gates, conv_w, a_log, dt_bias, norm_w, *, bsz, t, tt=256):
    n = proj.shape[0]
    nt = t // tt
    pad = LANES - N_GATE_COLS
    alog = jnp.pad(a_log.astype(F32), (GDN_H, pad)).reshape(1, LANES)
    dtb = jnp.pad(dt_bias.astype(F32), (GDN_H, pad)).reshape(1, LANES)
    rowmap = lambda b, i: (b * nt + i, 0)
    const = lambda b, i: (0, 0)
    return pl.pallas_call(
        functools.partial(_gdn_kernel, tt=tt),
        grid=(bsz, nt),
        in_specs=[
            pl.BlockSpec((tt, 3 * GDN_W), rowmap),
            pl.BlockSpec((tt, GDN_W), lambda b, i: (b * nt + i, OFF_Z // GDN_W)),
            pl.BlockSpec((tt, LANES), rowmap),
            pl.BlockSpec((GDN_K, 3 * GDN_W), const),
            pl.BlockSpec((1, LANES), const),
            pl.BlockSpec((1, LANES), const),
            pl.BlockSpec((1, GDN_D), const),
        ],
        out_specs=pl.BlockSpec((tt, GDN_W), rowmap),
        out_shape=jax.ShapeDtypeStruct((n, GDN_W), BF16),
        scratch_shapes=[
            pltpu.VMEM((tt, 3 * GDN_W), F32),
            pltpu.VMEM((SUBLANES, 3 * GDN_W), F32),
            pltpu.VMEM((GDN_H, GDN_D, GDN_D), F32),
        ],
        compiler_params=pltpu.CompilerParams(dimension_semantics=("arbitrary", "arbitrary"), vmem_limit_bytes=VMEM_LIMIT),
        name="gdn",
    )(proj, proj, gates, conv_w.astype(F32), alog, dtb, norm_w.astype(F32).reshape(1, GDN_D))


def _hgrn_sc_kernel(hq_ref, hf_ref, hi_ref, hg_ref, sb_ref, sc_ref, sh_ref, lbraw_ref, nw_ref, scw_ref,
                    o_ref, qs_ref, ks_ref, vs_ref, bs_ref, st_ref, carry_ref, *, tt, layer):
    c, d = HG_C, HG_D

    @pl.when(pl.program_id(1) == 0)
    def _():
        carry_ref[...] = jnp.zeros_like(carry_ref)
        st_ref[...] = jnp.zeros_like(st_ref)

    lbraw = lbraw_ref[...]
    e = jnp.exp(lbraw - jnp.max(lbraw, axis=0, keepdims=True))
    p = e / jnp.sum(e, axis=0, keepdims=True)
    lb = jnp.zeros((1, HG_W), F32)
    for kk in range(1, layer + 1):
        lb = lb + p[kk:kk + 1, :]

    fpre = hf_ref[...].astype(F32)
    log_sig = jnp.minimum(fpre, 0.0) - jnp.log1p(jnp.exp(-jnp.abs(fpre)))
    a = jnp.log(lb)
    cc = jnp.log1p(-lb) + log_sig
    log_f = jnp.maximum(a, cc) + jnp.log1p(jnp.exp(-jnp.abs(a - cc)))
    ri = lax.broadcasted_iota(jnp.int32, (tt, tt), 0)
    rj = lax.broadcasted_iota(jnp.int32, (tt, tt), 1)
    blocktril = (((ri // c) == (rj // c)) & (ri >= rj)).astype(BF16)
    bs_ref[...] = _dot_exact_lhs(blocktril, log_f)
    ks_ref[...] = (1.0 - lb) * _sigmoid(-fpre)
    qs_ref[...] = hq_ref[...].astype(F32)
    vs_ref[...] = hi_ref[...].astype(F32)

    prod = sc_ref[...].astype(F32) * sh_ref[...].astype(F32)
    xp = jnp.concatenate([carry_ref[...], prod], axis=0)
    base = SUBLANES - (SC_K - 1)
    conv = scw_ref[0:1, :] * xp[base:base + tt]
    for kk in range(1, SC_K):
        conv = conv + scw_ref[kk:kk + 1, :] * xp[base + kk:base + kk + tt]
    carry_ref[...] = prod[tt - SUBLANES:tt]
    o_ref[:, HG_W:HG_W + SC_W] = (sb_ref[...].astype(F32) * conv).astype(BF16)

    rowi = lax.broadcasted_iota(jnp.int32, (c, d), 0)
    ones = jnp.ones((d, d), BF16)
    nw = nw_ref[...]

    def chunk(ci, carry):
        r0 = pl.multiple_of(ci * c, c)
        rows = pl.ds(r0, c)
        heads = range(HG_H)
        cs = [slice(h * d, (h + 1) * d) for h in heads]
        q = [qs_ref[rows, s] for s in cs]
        k = [ks_ref[rows, s] for s in cs]
        v = [vs_ref[rows, s] for s in cs]
        b = [bs_ref[rows, s] for s in cs]
        bl = [a[c - 1:c, :] for a in b]
        st = [st_ref[h] for h in heads]
        o_inter = [_dot_nt((q[h] * jnp.exp(b[h])).astype(BF16), st[h].astype(BF16)) for h in heads]
        pmat = []
        for h in heads:
            parts = []
            for j in range(c):
                pj = q[h] * k[h][j:j + 1, :] * jnp.exp(jnp.minimum(b[h] - b[h][j:j + 1, :], 0.0))
                parts.append(jnp.where(rowi >= j, pj, 0.0))
            pmat.append(jnp.concatenate(parts, axis=0).astype(BF16))
        rsum = [_dot(p, ones) for p in pmat]
        o = []
        for h in heads:
            acc = o_inter[h] + rsum[h][0:c] * v[h][0:1, :]
            for j in range(1, c):
                acc = acc + rsum[h][j * c:(j + 1) * c] * v[h][j:j + 1, :]
            o.append(acc)
        kd = [(k[h] * jnp.exp(bl[h] - b[h])).astype(BF16) for h in heads]
        st_new = [st[h] * jnp.exp(bl[h]) + _dot_tn(v[h].astype(BF16), kd[h]) for h in heads]
        for h in heads:
            st_ref[h] = st_new[h]
        for h in heads:
            gate = hg_ref[rows, cs[h]].astype(F32)
            o_ref[rows, cs[h]] = (o[h] * _rms_scale(o[h]) * nw * _silu(gate)).astype(BF16)
        return carry

    lax.fori_loop(0, tt // c, chunk, 0)


def _hgrn_sc(proj, lb_raw, norm_w, sc_conv_w, *, bsz, t, layer, tt=256):
    n = proj.shape[0]
    nt = t // tt
    depth = lb_raw.shape[0]
    slab = lambda off: pl.BlockSpec((tt, HG_W), lambda b, i: (b * nt + i, off // HG_W))
    const = lambda b, i: (0, 0)
    return pl.pallas_call(
        functools.partial(_hgrn_sc_kernel, tt=tt, layer=layer),
        grid=(bsz, nt),
        in_specs=[slab(OFF_HQ), slab(OFF_HF), slab(OFF_HI), slab(OFF_HG), slab(OFF_SB), slab(OFF_SC), slab(OFF_SH),
                  pl.BlockSpec((depth, HG_W), const),
                  pl.BlockSpec((1, HG_D), const),
                  pl.BlockSpec((SC_K, SC_W), const)],
        out_specs=pl.BlockSpec((tt, HG_W + SC_W), lambda b, i: (b * nt + i, 0)),
        out_shape=jax.ShapeDtypeStruct((n, HG_W + SC_W), BF16),
        scratch_shapes=[pltpu.VMEM((tt, HG_W), F32)] * 4 + [
            pltpu.VMEM((HG_H, HG_D, HG_D), F32),
            pltpu.VMEM((SUBLANES, SC_W), F32),
        ],
        compiler_params=pltpu.CompilerParams(dimension_semantics=("arbitrary", "arbitrary"), vmem_limit_bytes=VMEM_LIMIT),
        name="hgrn_sc",
    )(proj, proj, proj, proj, proj, proj, proj, lb_raw.astype(F32), norm_w.astype(F32).reshape(1, HG_D),
      sc_conv_w.astype(F32))


def _outproj_router_kernel(ya_ref, ybc_ref, x_ref, wo_ref, nw_ref, wrh_ref, wrl_ref, br_ref,
                           x2_ref, eid_ref, wts_ref):
    tm = x_ref.shape[0]
    x2 = x_ref[...] + _dot(ya_ref[...], wo_ref[0:GDN_W, :]) + _dot(ybc_ref[...], wo_ref[GDN_W:, :])
    x2_ref[...] = x2
    h = x2 * _rms_scale(x2) * nw_ref[...]
    hh, hl = _split_hi_lo(h)
    wrh, wrl = wrh_ref[...], wrl_ref[...]
    logits = _dot(hh, wrh) + (_dot(hh, wrl) + _dot(hl, wrh)) + br_ref[...]

    lane_i = lax.broadcasted_iota(jnp.int32, (tm, LANES), 1)
    lane = lane_i.astype(F32)
    lane_grp = ((lane_i - N_GROUPS) // EPG).astype(F32)
    is_g = lane_i < N_GROUPS
    is_e = (lane_i >= N_GROUPS) & (lane_i < N_GROUPS + N_EXP)
    big = float(4 * LANES)

    lg = jnp.where(is_g, logits, -jnp.inf)
    mg = jnp.max(lg, axis=-1, keepdims=True)
    p_group = 1.0 / jnp.sum(jnp.exp(lg - mg), axis=-1, keepdims=True)
    gidx = jnp.min(jnp.where(lg == mg, lane, big), axis=-1, keepdims=True)

    le = jnp.where(is_e & (lane_grp == gidx), logits, -jnp.inf)
    m1 = jnp.max(le, axis=-1, keepdims=True)
    i1 = jnp.min(jnp.where(le == m1, lane, big), axis=-1, keepdims=True)
    le2 = jnp.where(lane == i1, -jnp.inf, le)
    m2 = jnp.max(le2, axis=-1, keepdims=True)
    i2 = jnp.min(jnp.where(le2 == m2, lane, big), axis=-1, keepdims=True)
    e2 = jnp.exp(m2 - m1)
    w1 = p_group / (1.0 + e2)
    w2 = p_group * e2 / (1.0 + e2)
    eid = jnp.where(lane_i == 0, i1 - N_GROUPS, jnp.where(lane_i == 1, i2 - N_GROUPS, 0.0))
    eid_ref[...] = eid.astype(jnp.int32)
    wts_ref[...] = jnp.where(lane_i == 0, w1, jnp.where(lane_i == 1, w2, 0.0))


def _outproj_router(ya, ybc, x, w_out, norm_w, wr_hi, wr_lo, b_r, *, tm=256):
    n, d = x.shape
    row = lambda i: (i, 0)
    const = lambda i: (0, 0)
    return pl.pallas_call(
        _outproj_router_kernel,
        grid=(n // tm,),
        in_specs=[pl.BlockSpec((tm, GDN_W), row), pl.BlockSpec((tm, HG_W + SC_W), row), pl.BlockSpec((tm, d), row),
                  pl.BlockSpec((d, d), const), pl.BlockSpec((1, d), const),
                  pl.BlockSpec((d, LANES), const), pl.BlockSpec((d, LANES), const), pl.BlockSpec((1, LANES), const)],
        out_specs=[pl.BlockSpec((tm, d), row), pl.BlockSpec((tm, LANES), row), pl.BlockSpec((tm, LANES), row)],
        out_shape=[jax.ShapeDtypeStruct((n, d), F32), jax.ShapeDtypeStruct((n, LANES), jnp.int32),
                   jax.ShapeDtypeStruct((n, LANES), F32)],
        compiler_params=pltpu.CompilerParams(dimension_semantics=("arbitrary",), vmem_limit_bytes=VMEM_LIMIT),
        name="outproj_router",
    )(ya, ybc, x, w_out, norm_w.reshape(1, d), wr_hi, wr_lo, b_r)


def _moe_plan(eid, wts, n_tok):
    blk = MOE_BLK
    n_assign = n_tok * TOP_K
    n_blocks = -(-n_assign // blk) + N_EXP
    n_rows = n_blocks * blk
    flat_e = eid.reshape(-1).astype(jnp.int32)
    flat_w = wts.reshape(-1)
    shift = max(1, (n_assign - 1).bit_length())
    keys = lax.sort(flat_e * (1 << shift) + jnp.arange(n_assign, dtype=jnp.int32))
    order = keys & ((1 << shift) - 1)
    experts = jnp.arange(N_EXP, dtype=jnp.int32)
    counts = jnp.sum((flat_e[:, None] == experts[None, :]).astype(jnp.int32), axis=0)
    starts = jnp.cumsum(counts) - counts
    padded = (counts + blk - 1) // blk * blk
    pad_end = jnp.cumsum(padded)
    pad_start = pad_end - padded
    r = jnp.arange(n_rows, dtype=jnp.int32)
    e_r = jnp.minimum(jnp.sum((r[:, None] >= pad_end[None, :]).astype(jnp.int32), axis=1), N_EXP - 1)
    j = r - pad_start[e_r]
    valid = (j < counts[e_r]) & (r < pad_end[-1])
    src = jnp.clip(starts[e_r] + j, 0, n_assign - 1)
    a = order[src]
    tok = a // TOP_K
    slot = a % TOP_K
    row_tok = jnp.where(valid, tok, 0).astype(jnp.int32)
    row_dst = jnp.where(valid, slot * n_tok + tok, TOP_K * n_tok + r % (2 * blk)).astype(jnp.int32)
    row_w = jnp.where(valid, flat_w[a], 0.0).astype(F32)
    n_active = (pad_end[-1] // blk).astype(jnp.int32)
    bidx = jnp.minimum(jnp.arange(n_blocks, dtype=jnp.int32), jnp.maximum(n_active - 1, 0))
    block_expert = e_r[bidx * blk]
    return (block_expert, n_active.reshape(1), row_tok.reshape(n_blocks, 1, blk), row_dst.reshape(n_blocks, 1, blk),
            row_w.reshape(n_blocks, blk, 1))


def _moe_kernel(bexp_ref, nact_ref, x_hbm, tokc_ref, tokn_ref, dst_ref, rw_ref, nw_ref, wg_ref, wu_ref, wd_ref,
                y_hbm, xbuf, ybuf, wgb, wub, wdb, gsem, ssem):
    blk = MOE_BLK
    b = pl.program_id(0)
    nact = nact_ref[0]
    slot = lax.rem(b, 2)
    other = 1 - slot
    active = b < nact

    def gather_start(tok_ref, s):
        for r in range(blk):
            pltpu.make_async_copy(x_hbm.at[pl.ds(tok_ref[0, r], 1), :], xbuf.at[s, pl.ds(r, 1), :], gsem.at[s]).start()

    def gather_wait(s):
        for r in range(blk):
            pltpu.make_async_copy(x_hbm.at[pl.ds(0, 1), :], xbuf.at[s, pl.ds(r, 1), :], gsem.at[s]).wait()

    def scatter_start(s):
        for r in range(blk):
            pltpu.make_async_copy(ybuf.at[s, pl.ds(r, 1), :], y_hbm.at[pl.ds(dst_ref[0, r], 1), :], ssem.at[s]).start()

    def scatter_wait(s):
        for r in range(blk):
            pltpu.make_async_copy(ybuf.at[s, pl.ds(r, 1), :], y_hbm.at[pl.ds(0, 1), :], ssem.at[s]).wait()

    @pl.when(b == 0)
    def _():
        gather_start(tokc_ref, 0)

    @pl.when(active & (b >= 2))
    def _():
        scatter_wait(slot)

    @pl.when(active & ((b == 0) | (bexp_ref[b] != bexp_ref[jnp.maximum(b - 1, 0)])))
    def _():
        wgb[...] = wg_ref[...].astype(BF16)
        wub[...] = wu_ref[...].astype(BF16)
        wdb[...] = wd_ref[...].astype(BF16)

    @pl.when(active)
    def _():
        gather_start(tokn_ref, other)
        gather_wait(slot)
        xv = xbuf[slot]
        hb = (xv * _rms_scale(xv) * nw_ref[...]).astype(BF16)
        g = _dot(hb, wgb[...])
        u = _dot(hb, wub[...])
        hid = (_silu(g) * u).astype(BF16)
        ybuf[slot] = _dot(hid, wdb[...]) * rw_ref[...]
        scatter_start(slot)

    @pl.when(b == nact - 1)
    def _():
        gather_wait(other)
        scatter_wait(slot)

        @pl.when(b >= 1)
        def _():
            scatter_wait(other)


def _moe_ffn(x2, plan, norm_w, w_gate, w_up, w_down, *, layer):
    n, d = x2.shape
    blk = MOE_BLK
    block_expert, n_active, row_tok, row_dst, row_w = plan
    n_blocks = row_tok.shape[0]
    wmap = lambda b, bexp, nact: (layer, bexp[b], 0, 0)
    grid_spec = pltpu.PrefetchScalarGridSpec(
        num_scalar_prefetch=2,
        grid=(n_blocks,),
        in_specs=[
            pl.BlockSpec(memory_space=pl.ANY),
            pl.BlockSpec((None, 1, blk), lambda b, bexp, nact: (b, 0, 0), memory_space=pltpu.SMEM),
            pl.BlockSpec((None, 1, blk), lambda b, bexp, nact: (jnp.minimum(b + 1, n_blocks - 1), 0, 0),
                         memory_space=pltpu.SMEM),
            pl.BlockSpec((None, 1, blk), lambda b, bexp, nact: (b, 0, 0), memory_space=pltpu.SMEM),
            pl.BlockSpec((None, blk, 1), lambda b, bexp, nact: (b, 0, 0)),
            pl.BlockSpec((1, d), lambda b, bexp, nact: (0, 0)),
            pl.BlockSpec((None, None, d, D_FF), wmap),
            pl.BlockSpec((None, None, d, D_FF), wmap),
            pl.BlockSpec((None, None, D_FF, d), wmap),
        ],
        out_specs=pl.BlockSpec(memory_space=pl.ANY),
        scratch_shapes=[pltpu.VMEM((2, blk, d), F32), pltpu.VMEM((2, blk, d), F32),
                        pltpu.VMEM((d, D_FF), BF16), pltpu.VMEM((d, D_FF), BF16), pltpu.VMEM((D_FF, d), BF16),
                        pltpu.SemaphoreType.DMA((2,)), pltpu.SemaphoreType.DMA((2,))],
    )
    return pl.pallas_call(
        _moe_kernel,
        grid_spec=grid_spec,
        out_shape=jax.ShapeDtypeStruct((TOP_K * n + 2 * blk, d), F32),
        compiler_params=pltpu.CompilerParams(dimension_semantics=("arbitrary",), vmem_limit_bytes=VMEM_LIMIT),
        name="moe_ffn",
    )(block_expert, n_active, x2, row_tok, row_tok, row_dst, row_w, norm_w.reshape(1, d), w_gate, w_up, w_down)


def _split_w_in(w_in_l):
    g0 = 4 * GDN_W
    w_main = jnp.concatenate([w_in_l[:, :g0], w_in_l[:, g0 + N_GATE_COLS:]], axis=1).astype(BF16)
    w_gates = jnp.pad(w_in_l[:, g0:g0 + N_GATE_COLS], ((0, 0), (0, LANES - N_GATE_COLS))).astype(BF16)
    return w_main, w_gates


def _router_weights(w_rg, b_rg, w_re, b_re):
    d = w_rg.shape[0]
    pad = LANES - N_GROUPS - N_EXP
    w_r = jnp.concatenate([w_rg, w_re, jnp.zeros((d, pad), F32)], axis=1).astype(F32)
    b_r = jnp.concatenate([b_rg, b_re, jnp.zeros((pad,), F32)]).astype(F32).reshape(1, LANES)
    hi, lo = _split_hi_lo(w_r)
    return hi, lo, b_r


def kernel(x, attn_norm_w, w_in, gdn_conv_w, gdn_a_log, gdn_dt_bias, gdn_norm_w, hgrn_lower_bounds, hgrn_norm_w, sc_conv_w, w_out, ffn_norm_w, w_router_group, b_router_group, w_router_expert, b_router_expert, w_gate, w_up, w_down, final_norm_w):
    bsz, t, d = x.shape
    n = bsz * t
    depth = w_in.shape[0]
    xs = x.reshape(n, d).astype(F32)
    y2 = None
    for l in range(depth):
        w_main, w_gates = _split_w_in(w_in[l])
        if y2 is None:
            h, gates = _combine_norm(xs, None, attn_norm_w[l], w_gates, final=False)
        else:
            xs, h, gates = _combine_norm(xs, y2, attn_norm_w[l], w_gates, final=False)
        proj = _inproj(h, w_main)
        ya = _gdn(proj, gates, gdn_conv_w[l], gdn_a_log[l], gdn_dt_bias[l], gdn_norm_w[l], bsz=bsz, t=t)
        ybc = _hgrn_sc(proj, hgrn_lower_bounds, hgrn_norm_w[l], sc_conv_w[l], bsz=bsz, t=t, layer=l)
        wr_hi, wr_lo, b_r = _router_weights(w_router_group[l], b_router_group[l], w_router_expert[l], b_router_expert[l])
        xs, eid, wts = _outproj_router(ya, ybc, xs, w_out[l].astype(BF16), ffn_norm_w[l], wr_hi, wr_lo, b_r)
        plan = _moe_plan(eid[:, :TOP_K], wts[:, :TOP_K], n)
        y2 = _moe_ffn(xs, plan, ffn_norm_w[l], w_gate, w_up, w_down, layer=l)
    (out,) = _combine_norm(xs, y2, final_norm_w, None, final=True)
    return out.reshape(bsz, t, d).astype(x.dtype)
```

```python
import functools

import jax
import jax.numpy as jnp
from jax import lax
from jax.experimental import pallas as pl
from jax.experimental.pallas import tpu as pltpu

F32 = jnp.float32
BF16 = jnp.bfloat16
EPS = 1e-6

GDN_H, GDN_D, GDN_C, GDN_K = 8, 128, 64, 4
GDN_W = GDN_H * GDN_D
HG_H, HG_D, HG_C = 4, 128, 16
HG_W = HG_H * HG_D
SC_W, SC_K = 512, 3
N_GROUPS, EPG, N_EXP, D_FF, TOP_K = 8, 8, 64, 512, 2
MOE_BLK = 256
N_GATE_COLS = 2 * GDN_H

LANES = 128
SUBLANES = 8
VMEM_LIMIT = 56 * 1024 * 1024

PROJ_W = 4 * GDN_W + 2 * HG_W + 2 * HG_W + 3 * SC_W
OFF_QKV, OFF_Z = 0, 3 * GDN_W
OFF_HQ = 4 * GDN_W
OFF_HF, OFF_HI, OFF_HG = OFF_HQ + HG_W, OFF_HQ + 2 * HG_W, OFF_HQ + 3 * HG_W
OFF_SB = OFF_HQ + 4 * HG_W
OFF_SC, OFF_SH = OFF_SB + SC_W, OFF_SB + 2 * SC_W


def _dot(a, b):
    return jnp.dot(a, b, preferred_element_type=F32)


def _dot_nt(a, b):
    return lax.dot_general(a, b, (((1,), (1,)), ((), ())), preferred_element_type=F32)


def _dot_tn(a, b):
    return lax.dot_general(a, b, (((0,), (0,)), ((), ())), preferred_element_type=F32)


def _split_hi_lo(x):
    hi = x.astype(BF16)
    lo = (x - hi.astype(F32)).astype(BF16)
    return hi, lo


def _dot_exact_lhs(mask_bf16, x):
    hi, lo = _split_hi_lo(x)
    return _dot(mask_bf16, hi) + _dot(mask_bf16, lo)


def _sigmoid(x):
    return 1.0 / (1.0 + jnp.exp(-x))


def _silu(x):
    return x * _sigmoid(x)


def _softplus(x):
    return jnp.maximum(x, 0.0) + jnp.log1p(jnp.exp(-jnp.abs(x)))


def _rms_scale(x):
    return lax.rsqrt(jnp.mean(x * x, axis=-1, keepdims=True) + EPS)


def _combine_norm_kernel(*refs, n_add, write_x, final):
    it = iter(refs)
    x_ref = next(it)
    add_refs = [next(it) for _ in range(n_add)]
    xv = x_ref[...]
    if n_add:
        wts = next(it)[...]
        for k, a in enumerate(add_refs):
            xv = xv + a[...] * wts[:, k:k + 1]
    nw_ref = next(it)
    wg_ref = None if final else next(it)
    xo_ref = next(it) if write_x else None
    if write_x:
        xo_ref[...] = xv
    hn = xv * _rms_scale(xv) * nw_ref[...]
    if final:
        next(it)[...] = hn
    else:
        h_ref, g_ref = next(it), next(it)
        hb = hn.astype(BF16)
        h_ref[...] = hb
        g_ref[...] = _dot(hb, wg_ref[...])


def _combine_norm(x, y2, wts, norm_w, w_gates, *, final, tm=256):
    n, d = x.shape
    n_add = 0 if y2 is None else 2
    write_x = (n_add > 0) and not final
    row = lambda i: (i, 0)
    in_specs = [pl.BlockSpec((tm, d), row)]
    args = [x]
    if n_add:
        in_specs += [pl.BlockSpec((tm, d), row), pl.BlockSpec((tm, d), lambda i: (n // tm + i, 0)),
                     pl.BlockSpec((tm, LANES), row)]
        args += [y2, y2, wts]
    in_specs.append(pl.BlockSpec((1, d), lambda i: (0, 0)))
    args.append(norm_w.reshape(1, d))
    out_shape, out_specs = [], []
    if not final:
        in_specs.append(pl.BlockSpec((d, LANES), lambda i: (0, 0)))
        args.append(w_gates)
    if write_x:
        out_shape.append(jax.ShapeDtypeStruct((n, d), F32))
        out_specs.append(pl.BlockSpec((tm, d), row))
    if final:
        out_shape.append(jax.ShapeDtypeStruct((n, d), F32))
        out_specs.append(pl.BlockSpec((tm, d), row))
    else:
        out_shape += [jax.ShapeDtypeStruct((n, d), BF16), jax.ShapeDtypeStruct((n, LANES), F32)]
        out_specs += [pl.BlockSpec((tm, d), row), pl.BlockSpec((tm, LANES), row)]
    return pl.pallas_call(
        functools.partial(_combine_norm_kernel, n_add=n_add, write_x=write_x, final=final),
        grid=(n // tm,),
        in_specs=in_specs,
        out_specs=out_specs,
        out_shape=out_shape,
        compiler_params=pltpu.CompilerParams(dimension_semantics=("arbitrary",), vmem_limit_bytes=VMEM_LIMIT),
        name="combine_norm_final" if final else "combine_norm",
    )(*args)


def _inproj_kernel(h_ref, w_ref, o_ref):
    o_ref[...] = _dot(h_ref[...], w_ref[...]).astype(BF16)


def _inproj(h, w_main, *, tm=1024, tn=1280):
    n, d = h.shape
    pw = w_main.shape[1]
    return pl.pallas_call(
        _inproj_kernel,
        grid=(pw // tn, n // tm),
        in_specs=[pl.BlockSpec((tm, d), lambda j, i: (i, 0)), pl.BlockSpec((d, tn), lambda j, i: (0, j))],
        out_specs=pl.BlockSpec((tm, tn), lambda j, i: (i, j)),
        out_shape=jax.ShapeDtypeStruct((n, pw), BF16),
        compiler_params=pltpu.CompilerParams(dimension_semantics=("arbitrary", "arbitrary"), vmem_limit_bytes=VMEM_LIMIT),
        name="inproj",
    )(h, w_main)


def _unit_lower_inverse_minus_eye(lowers, eye, bd16, off32, off64):
    mm = lambda a, b: _dot(a.astype(BF16), b.astype(BF16))
    ld = [jnp.where(bd16, m, 0.0) for m in lowers]
    l2 = [mm(a, a) for a in ld]
    l4 = [mm(a, a) for a in l2]
    l8 = [mm(a, a) for a in l4]
    pa = [mm(eye - a, eye + b) for a, b in zip(ld, l2)]
    pb = [mm(eye + a, eye + b) for a, b in zip(l4, l8)]
    t16 = [mm(a, b) for a, b in zip(pa, pb)]
    x1 = [mm(t, jnp.where(off32, m, 0.0)) for t, m in zip(t16, lowers)]
    t32 = [t - mm(x, t) for t, x in zip(t16, x1)]
    x2 = [mm(t, jnp.where(off64, m, 0.0)) for t, m in zip(t32, lowers)]
    return [t - mm(x, t) - eye for t, x in zip(t32, x2)]


def _gdn_kernel(qkv_ref, z_ref, gat_ref, cw_ref, alog_ref, dtb_ref, nw_ref, o_ref,
                xs_ref, carry_ref, s_ref, *, tt):
    c, d, w = GDN_C, GDN_D, GDN_W

    @pl.when(pl.program_id(1) == 0)
    def _():
        carry_ref[...] = jnp.zeros_like(carry_ref)
        s_ref[...] = jnp.zeros_like(s_ref)

    x = qkv_ref[...].astype(F32)
    xp = jnp.concatenate([carry_ref[...], x], axis=0)
    base = SUBLANES - (GDN_K - 1)
    acc = cw_ref[0:1, :] * xp[base:base + tt]
    for k in range(1, GDN_K):
        acc = acc + cw_ref[k:k + 1, :] * xp[base + k:base + k + tt]
    carry_ref[...] = x[tt - SUBLANES:tt]
    xs_ref[...] = _silu(acc)

    row = lax.broadcasted_iota(jnp.int32, (c, c), 0)
    col = lax.broadcasted_iota(jnp.int32, (c, c), 1)
    causal = row >= col
    strict = row > col
    eye = (row == col).astype(F32)
    bd16 = (row // 16) == (col // 16)
    off32 = ((row // 32) == (col // 32)) & ((row // 16) != (col // 16))
    off64 = (row // 32) != (col // 32)
    tril = causal.astype(BF16)
    alog = alog_ref[...]
    dtb = dtb_ref[...]
    nw = nw_ref[...]

    def chunk(ci):
        rows = pl.ds(ci * c, c)
        gat = gat_ref[rows, :]
        beta_all = _sigmoid(gat)
        g_all = -jnp.exp(alog) * _softplus(gat + dtb)
        gc_all = _dot_exact_lhs(tril, g_all)
        gc_t = jnp.concatenate([gc_all, jnp.zeros((LANES - c, LANES), F32)], axis=0).T
        heads = range(GDN_H)
        q = [xs_ref[rows, h * d:(h + 1) * d] for h in heads]
        k = [xs_ref[rows, w + h * d:w + (h + 1) * d] for h in heads]
        v = [xs_ref[rows, 2 * w + h * d:2 * w + (h + 1) * d] for h in heads]
        qn = [a * lax.rsqrt(jnp.sum(a * a, axis=-1, keepdims=True) + EPS) * (d ** -0.5) for a in q]
        kn = [a * lax.rsqrt(jnp.sum(a * a, axis=-1, keepdims=True) + EPS) for a in k]
        beta = [beta_all[:, h:h + 1] for h in heads]
        gc = [gc_all[:, GDN_H + h:GDN_H + h + 1] for h in heads]
        gl = [a[c - 1:c, :] for a in gc]
        decay = [jnp.exp(jnp.where(causal, gc[h] - gc_t[GDN_H + h:GDN_H + h + 1, 0:c], -jnp.inf)) for h in heads]
        egc = [jnp.exp(a) for a in gc]
        kb = [a * b for a, b in zip(kn, beta)]
        knb = [a.astype(BF16) for a in kn]
        kk = [_dot_nt(a.astype(BF16), b) for a, b in zip(kb, knb)]
        qk = [_dot_nt(a.astype(BF16), b) for a, b in zip(qn, knb)]
        lower = [jnp.where(strict, a * b, 0.0) for a, b in zip(kk, decay)]
        attn = [(a * b).astype(BF16) for a, b in zip(qk, decay)]
        tm1 = _unit_lower_inverse_minus_eye(lower, eye, bd16, off32, off64)
        rhs = [jnp.concatenate([v[h] * beta[h], kb[h] * egc[h]], axis=1) for h in heads]
        sol = [r + _dot(t.astype(BF16), r.astype(BF16)) for t, r in zip(tm1, rhs)]
        qd = [(qn[h] * egc[h]).astype(BF16) for h in heads]
        kd = [(kn[h] * jnp.exp(gl[h] - gc[h])).astype(BF16) for h in heads]
        s = [s_ref[h] for h in heads]
        sb = [a.astype(BF16) for a in s]
        v_new = [sol[h][:, :d] - _dot(sol[h][:, d:].astype(BF16), sb[h]) for h in heads]
        vnb = [a.astype(BF16) for a in v_new]
        o = [_dot(qd[h], sb[h]) + _dot(attn[h], vnb[h]) for h in heads]
        s_new = [s[h] * jnp.exp(gl[h]) + _dot_tn(kd[h], vnb[h]) for h in heads]
        for h in heads:
            s_ref[h] = s_new[h]
        for h in heads:
            zz = z_ref[rows, h * d:(h + 1) * d].astype(F32)
            o_ref[rows, h * d:(h + 1) * d] = (o[h] * _rms_scale(o[h]) * nw * _silu(zz)).astype(BF16)

    for ci in range(tt // c):
        chunk(ci)


def _gdn(proj, gates, conv_w, a_log, dt_bias, norm_w, *, bsz, t, tt=256):
    n = proj.shape[0]
    nt = t // tt
    pad = LANES - N_GATE_COLS
    alog = jnp.pad(a_log.astype(F32), (GDN_H, pad)).reshape(1, LANES)
    dtb = jnp.pad(dt_bias.astype(F32), (GDN_H, pad)).reshape(1, LANES)
    rowmap = lambda b, i: (b * nt + i, 0)
    const = lambda b, i: (0, 0)
    return pl.pallas_call(
        functools.partial(_gdn_kernel, tt=tt),
        grid=(bsz, nt),
        in_specs=[
            pl.BlockSpec((tt, 3 * GDN_W), rowmap),
            pl.BlockSpec((tt, GDN_W), lambda b, i: (b * nt + i, OFF_Z // GDN_W)),
            pl.BlockSpec((tt, LANES), rowmap),
            pl.BlockSpec((GDN_K, 3 * GDN_W), const),
            pl.BlockSpec((1, LANES), const),
            pl.BlockSpec((1, LANES), const),
            pl.BlockSpec((1, GDN_D), const),
        ],
        out_specs=pl.BlockSpec((tt, GDN_W), rowmap),
        out_shape=jax.ShapeDtypeStruct((n, GDN_W), BF16),
        scratch_shapes=[
            pltpu.VMEM((tt, 3 * GDN_W), F32),
            pltpu.VMEM((SUBLANES, 3 * GDN_W), F32),
            pltpu.VMEM((GDN_H, GDN_D, GDN_D), F32),
        ],
        compiler_params=pltpu.CompilerParams(dimension_semantics=("arbitrary", "arbitrary"), vmem_limit_bytes=VMEM_LIMIT),
        name="gdn",
    )(proj, proj, gates, conv_w.astype(F32), alog, dtb, norm_w.astype(F32).reshape(1, GDN_D))


def _hgrn_sc_kernel(hq_ref, hf_ref, hi_ref, hg_ref, sb_ref, sc_ref, sh_ref, lbraw_ref, nw_ref, scw_ref,
                    o_ref, qs_ref, ks_ref, vs_ref, bs_ref, st_ref, carry_ref, *, tt, layer):
    c, d = HG_C, HG_D

    @pl.when(pl.program_id(1) == 0)
    def _():
        carry_ref[...] = jnp.zeros_like(carry_ref)
        st_ref[...] = jnp.zeros_like(st_ref)

    lbraw = lbraw_ref[...]
    e = jnp.exp(lbraw - jnp.max(lbraw, axis=0, keepdims=True))
    p = e / jnp.sum(e, axis=0, keepdims=True)
    lb = jnp.zeros((1, HG_W), F32)
    for kk in range(1, layer + 1):
        lb = lb + p[kk:kk + 1, :]

    fpre = hf_ref[...].astype(F32)
    log_sig = jnp.minimum(fpre, 0.0) - jnp.log1p(jnp.exp(-jnp.abs(fpre)))
    a = jnp.log(lb)
    cc = jnp.log1p(-lb) + log_sig
    log_f = jnp.maximum(a, cc) + jnp.log1p(jnp.exp(-jnp.abs(a - cc)))
    ri = lax.broadcasted_iota(jnp.int32, (tt, tt), 0)
    rj = lax.broadcasted_iota(jnp.int32, (tt, tt), 1)
    blocktril = (((ri // c) == (rj // c)) & (ri >= rj)).astype(BF16)
    bs_ref[...] = _dot_exact_lhs(blocktril, log_f)
    ks_ref[...] = (1.0 - lb) * _sigmoid(-fpre)
    qs_ref[...] = hq_ref[...].astype(F32)
    vs_ref[...] = hi_ref[...].astype(F32)

    prod = sc_ref[...].astype(F32) * sh_ref[...].astype(F32)
    xp = jnp.concatenate([carry_ref[...], prod], axis=0)
    base = SUBLANES - (SC_K - 1)
    conv = scw_ref[0:1, :] * xp[base:base + tt]
    for kk in range(1, SC_K):
        conv = conv + scw_ref[kk:kk + 1, :] * xp[base + kk:base + kk + tt]
    carry_ref[...] = prod[tt - SUBLANES:tt]
    o_ref[:, HG_W:HG_W + SC_W] = (sb_ref[...].astype(F32) * conv).astype(BF16)

    rowi = lax.broadcasted_iota(jnp.int32, (c, d), 0)
    ones = jnp.ones((d, d), BF16)
    nw = nw_ref[...]

    def chunk(ci, carry):
        r0 = pl.multiple_of(ci * c, c)
        rows = pl.ds(r0, c)
        heads = range(HG_H)
        cs = [slice(h * d, (h + 1) * d) for h in heads]
        q = [qs_ref[rows, s] for s in cs]
        k = [ks_ref[rows, s] for s in cs]
        v = [vs_ref[rows, s] for s in cs]
        b = [bs_ref[rows, s] for s in cs]
        bl = [a[c - 1:c, :] for a in b]
        st = [st_ref[h] for h in heads]
        o_inter = [_dot_nt((q[h] * jnp.exp(b[h])).astype(BF16), st[h].astype(BF16)) for h in heads]
        pmat = []
        for h in heads:
            parts = []
            for j in range(c):
                pj = q[h] * k[h][j:j + 1, :] * jnp.exp(jnp.minimum(b[h] - b[h][j:j + 1, :], 0.0))
                parts.append(jnp.where(rowi >= j, pj, 0.0))
            pmat.append(jnp.concatenate(parts, axis=0).astype(BF16))
        rsum = [_dot(p, ones) for p in pmat]
        o = []
        for h in heads:
            acc = o_inter[h] + rsum[h][0:c] * v[h][0:1, :]
            for j in range(1, c):
                acc = acc + rsum[h][j * c:(j + 1) * c] * v[h][j:j + 1, :]
            o.append(acc)
        kd = [(k[h] * jnp.exp(bl[h] - b[h])).astype(BF16) for h in heads]
        st_new = [st[h] * jnp.exp(bl[h]) + _dot_tn(v[h].astype(BF16), kd[h]) for h in heads]
        for h in heads:
            st_ref[h] = st_new[h]
        for h in heads:
            gate = hg_ref[rows, cs[h]].astype(F32)
            o_ref[rows, cs[h]] = (o[h] * _rms_scale(o[h]) * nw * _silu(gate)).astype(BF16)
        return carry

    lax.fori_loop(0, tt // c, chunk, 0)


def _hgrn_sc(proj, lb_raw, norm_w, sc_conv_w, *, bsz, t, layer, tt=256):
    n = proj.shape[0]
    nt = t // tt
    depth = lb_raw.shape[0]
    slab = lambda off: pl.BlockSpec((tt, HG_W), lambda b, i: (b * nt + i, off // HG_W))
    const = lambda b, i: (0, 0)
    return pl.pallas_call(
        functools.partial(_hgrn_sc_kernel, tt=tt, layer=layer),
        grid=(bsz, nt),
        in_specs=[slab(OFF_HQ), slab(OFF_HF), slab(OFF_HI), slab(OFF_HG), slab(OFF_SB), slab(OFF_SC), slab(OFF_SH),
                  pl.BlockSpec((depth, HG_W), const),
                  pl.BlockSpec((1, HG_D), const),
                  pl.BlockSpec((SC_K, SC_W), const)],
        out_specs=pl.BlockSpec((tt, HG_W + SC_W), lambda b, i: (b * nt + i, 0)),
        out_shape=jax.ShapeDtypeStruct((n, HG_W + SC_W), BF16),
        scratch_shapes=[pltpu.VMEM((tt, HG_W), F32)] * 4 + [
            pltpu.VMEM((HG_H, HG_D, HG_D), F32),
            pltpu.VMEM((SUBLANES, SC_W), F32),
        ],
        compiler_params=pltpu.CompilerParams(dimension_semantics=("arbitrary", "arbitrary"), vmem_limit_bytes=VMEM_LIMIT),
        name="hgrn_sc",
    )(proj, proj, proj, proj, proj, proj, proj, lb_raw.astype(F32), norm_w.astype(F32).reshape(1, HG_D),
      sc_conv_w.astype(F32))


def _outproj_router_kernel(ya_ref, ybc_ref, x_ref, wo_ref, nw_ref, wrh_ref, wrl_ref, br_ref,
                           x2_ref, eid_ref, wts_ref):
    tm = x_ref.shape[0]
    x2 = x_ref[...] + _dot(ya_ref[...], wo_ref[0:GDN_W, :]) + _dot(ybc_ref[...], wo_ref[GDN_W:, :])
    x2_ref[...] = x2
    h = x2 * _rms_scale(x2) * nw_ref[...]
    hh, hl = _split_hi_lo(h)
    wrh, wrl = wrh_ref[...], wrl_ref[...]
    logits = _dot(hh, wrh) + (_dot(hh, wrl) + _dot(hl, wrh)) + br_ref[...]

    lane_i = lax.broadcasted_iota(jnp.int32, (tm, LANES), 1)
    lane = lane_i.astype(F32)
    lane_grp = ((lane_i - N_GROUPS) // EPG).astype(F32)
    is_g = lane_i < N_GROUPS
    is_e = (lane_i >= N_GROUPS) & (lane_i < N_GROUPS + N_EXP)
    big = float(4 * LANES)

    lg = jnp.where(is_g, logits, -jnp.inf)
    mg = jnp.max(lg, axis=-1, keepdims=True)
    p_group = 1.0 / jnp.sum(jnp.exp(lg - mg), axis=-1, keepdims=True)
    gidx = jnp.min(jnp.where(lg == mg, lane, big), axis=-1, keepdims=True)

    le = jnp.where(is_e & (lane_grp == gidx), logits, -jnp.inf)
    m1 = jnp.max(le, axis=-1, keepdims=True)
    i1 = jnp.min(jnp.where(le == m1, lane, big), axis=-1, keepdims=True)
    le2 = jnp.where(lane == i1, -jnp.inf, le)
    m2 = jnp.max(le2, axis=-1, keepdims=True)
    i2 = jnp.min(jnp.where(le2 == m2, lane, big), axis=-1, keepdims=True)
    e2 = jnp.exp(m2 - m1)
    w1 = p_group / (1.0 + e2)
    w2 = p_group * e2 / (1.0 + e2)
    eid = jnp.where(lane_i == 0, i1 - N_GROUPS, jnp.where(lane_i == 1, i2 - N_GROUPS, 0.0))
    eid_ref[...] = eid.astype(jnp.int32)
    wts_ref[...] = jnp.where(lane_i == 0, w1, jnp.where(lane_i == 1, w2, 0.0))


def _outproj_router(ya, ybc, x, w_out, norm_w, wr_hi, wr_lo, b_r, *, tm=256):
    n, d = x.shape
    row = lambda i: (i, 0)
    const = lambda i: (0, 0)
    return pl.pallas_call(
        _outproj_router_kernel,
        grid=(n // tm,),
        in_specs=[pl.BlockSpec((tm, GDN_W), row), pl.BlockSpec((tm, HG_W + SC_W), row), pl.BlockSpec((tm, d), row),
                  pl.BlockSpec((d, d), const), pl.BlockSpec((1, d), const),
                  pl.BlockSpec((d, LANES), const), pl.BlockSpec((d, LANES), const), pl.BlockSpec((1, LANES), const)],
        out_specs=[pl.BlockSpec((tm, d), row), pl.BlockSpec((tm, LANES), row), pl.BlockSpec((tm, LANES), row)],
        out_shape=[jax.ShapeDtypeStruct((n, d), F32), jax.ShapeDtypeStruct((n, LANES), jnp.int32),
                   jax.ShapeDtypeStruct((n, LANES), F32)],
        compiler_params=pltpu.CompilerParams(dimension_semantics=("arbitrary",), vmem_limit_bytes=VMEM_LIMIT),
        name="outproj_router",
    )(ya, ybc, x, w_out, norm_w.reshape(1, d), wr_hi, wr_lo, b_r)


def _moe_plan(eid, n_tok):
    blk = MOE_BLK
    n_assign = n_tok * TOP_K
    n_blocks = -(-n_assign // blk) + N_EXP
    flat_e = eid.reshape(-1).astype(jnp.int32)
    shift = max(1, (n_assign - 1).bit_length())
    keys = lax.sort(flat_e * (1 << shift) + jnp.arange(n_assign, dtype=jnp.int32))
    order = keys & ((1 << shift) - 1)
    experts = jnp.arange(N_EXP, dtype=jnp.int32)
    counts = jnp.sum((flat_e[:, None] == experts[None, :]).astype(jnp.int32), axis=0)
    starts = jnp.cumsum(counts) - counts
    padded = (counts + blk - 1) // blk * blk
    pad_end = jnp.cumsum(padded)
    pad_start = pad_end - padded
    n_active = (pad_end[-1] // blk).astype(jnp.int32)
    bidx = jnp.arange(n_blocks, dtype=jnp.int32)
    bstart = bidx * blk
    e_b = jnp.minimum(jnp.sum((bstart[:, None] >= pad_end[None, :]).astype(jnp.int32), axis=1), N_EXP - 1)
    j0 = bstart - pad_start[e_b]
    nvalid = jnp.clip(counts[e_b] - j0, 0, blk)
    i = jnp.arange(blk, dtype=jnp.int32)[None, :]
    valid = i < nvalid[:, None]
    src = jnp.clip((starts[e_b] + j0)[:, None] + i, 0, n_assign - 1)
    a = order[src]
    tok = a // TOP_K
    slot = a % TOP_K
    row_tok = jnp.where(valid, tok, 0).astype(jnp.int32)
    trash = TOP_K * n_tok + (bidx % 2)[:, None] * blk + i
    row_dst = jnp.where(valid, slot * n_tok + tok, trash).astype(jnp.int32)
    block_expert = e_b[jnp.minimum(bidx, jnp.maximum(n_active - 1, 0))]
    return block_expert, n_active.reshape(1), row_tok[:, None, :], row_dst[:, None, :]


def _moe_kernel(bexp_ref, nact_ref, x_hbm, tokc_ref, tokn_ref, dst_ref, nw_ref, wg_ref, wu_ref, wd_ref,
                y_hbm, xbuf, ybuf, wgb, wub, wdb, gsem, ssem):
    blk = MOE_BLK
    b = pl.program_id(0)
    nact = nact_ref[0]
    slot = lax.rem(b, 2)
    other = 1 - slot
    active = b < nact

    def gather_start(tok_ref, s):
        for r in range(blk):
            pltpu.make_async_copy(x_hbm.at[pl.ds(tok_ref[0, r], 1), :], xbuf.at[s, pl.ds(r, 1), :], gsem.at[s]).start()

    def gather_wait(s):
        for r in range(blk):
            pltpu.make_async_copy(x_hbm.at[pl.ds(0, 1), :], xbuf.at[s, pl.ds(r, 1), :], gsem.at[s]).wait()

    def scatter_start(s):
        for r in range(blk):
            pltpu.make_async_copy(ybuf.at[s, pl.ds(r, 1), :], y_hbm.at[pl.ds(dst_ref[0, r], 1), :], ssem.at[s]).start()

    def scatter_wait(s):
        for r in range(blk):
            pltpu.make_async_copy(ybuf.at[s, pl.ds(r, 1), :], y_hbm.at[pl.ds(0, 1), :], ssem.at[s]).wait()

    @pl.when(b == 0)
    def _():
        gather_start(tokc_ref, 0)

    @pl.when(active & (b >= 2))
    def _():
        scatter_wait(slot)

    @pl.when(active & ((b == 0) | (bexp_ref[b] != bexp_ref[jnp.maximum(b - 1, 0)])))
    def _():
        wgb[...] = wg_ref[...].astype(BF16)
        wub[...] = wu_ref[...].astype(BF16)
        wdb[...] = wd_ref[...].astype(BF16)

    @pl.when(active)
    def _():
        gather_start(tokn_ref, other)
        gather_wait(slot)
        xv = xbuf[slot]
        hb = (xv * _rms_scale(xv) * nw_ref[...]).astype(BF16)
        g = _dot(hb, wgb[...])
        u = _dot(hb, wub[...])
        hid = (_silu(g) * u).astype(BF16)
        ybuf[slot] = _dot(hid, wdb[...])
        scatter_start(slot)

    @pl.when(b == nact - 1)
    def _():
        gather_wait(other)
        scatter_wait(slot)

        @pl.when(b >= 1)
        def _():
            scatter_wait(other)


def _moe_ffn(x2, plan, norm_w, w_gate, w_up, w_down, *, layer):
    n, d = x2.shape
    blk = MOE_BLK
    block_expert, n_active, row_tok, row_dst = plan
    n_blocks = row_tok.shape[0]
    wmap = lambda b, bexp, nact: (layer, bexp[b], 0, 0)
    grid_spec = pltpu.PrefetchScalarGridSpec(
        num_scalar_prefetch=2,
        grid=(n_blocks,),
        in_specs=[
            pl.BlockSpec(memory_space=pl.ANY),
            pl.BlockSpec((None, 1, blk), lambda b, bexp, nact: (b, 0, 0), memory_space=pltpu.SMEM),
            pl.BlockSpec((None, 1, blk), lambda b, bexp, nact: (jnp.minimum(b + 1, n_blocks - 1), 0, 0),
                         memory_space=pltpu.SMEM),
            pl.BlockSpec((None, 1, blk), lambda b, bexp, nact: (b, 0, 0), memory_space=pltpu.SMEM),
            pl.BlockSpec((1, d), lambda b, bexp, nact: (0, 0)),
            pl.BlockSpec((None, None, d, D_FF), wmap),
            pl.BlockSpec((None, None, d, D_FF), wmap),
            pl.BlockSpec((None, None, D_FF, d), wmap),
        ],
        out_specs=pl.BlockSpec(memory_space=pl.ANY),
        scratch_shapes=[pltpu.VMEM((2, blk, d), F32), pltpu.VMEM((2, blk, d), F32),
                        pltpu.VMEM((d, D_FF), BF16), pltpu.VMEM((d, D_FF), BF16), pltpu.VMEM((D_FF, d), BF16),
                        pltpu.SemaphoreType.DMA((2,)), pltpu.SemaphoreType.DMA((2,))],
    )
    return pl.pallas_call(
        _moe_kernel,
        grid_spec=grid_spec,
        out_shape=jax.ShapeDtypeStruct((TOP_K * n + 2 * blk, d), F32),
        compiler_params=pltpu.CompilerParams(dimension_semantics=("arbitrary",), vmem_limit_bytes=VMEM_LIMIT),
        name="moe_ffn",
    )(block_expert, n_active, x2, row_tok, row_tok, row_dst, norm_w.reshape(1, d), w_gate, w_up, w_down)


def _split_w_in(w_in_l):
    g0 = 4 * GDN_W
    w_main = jnp.concatenate([w_in_l[:, :g0], w_in_l[:, g0 + N_GATE_COLS:]], axis=1).astype(BF16)
    w_gates = jnp.pad(w_in_l[:, g0:g0 + N_GATE_COLS], ((0, 0), (0, LANES - N_GATE_COLS))).astype(BF16)
    return w_main, w_gates


def _router_weights(w_rg, b_rg, w_re, b_re):
    d = w_rg.shape[0]
    pad = LANES - N_GROUPS - N_EXP
    w_r = jnp.concatenate([w_rg, w_re, jnp.zeros((d, pad), F32)], axis=1).astype(F32)
    b_r = jnp.concatenate([b_rg, b_re, jnp.zeros((pad,), F32)]).astype(F32).reshape(1, LANES)
    hi, lo = _split_hi_lo(w_r)
    return hi, lo, b_r


def kernel(x, attn_norm_w, w_in, gdn_conv_w, gdn_a_log, gdn_dt_bias, gdn_norm_w, hgrn_lower_bounds, hgrn_norm_w, sc_conv_w, w_out, ffn_norm_w, w_router_group, b_router_group, w_router_expert, b_router_expert, w_gate, w_up, w_down, final_norm_w):
    bsz, t, d = x.shape
    n = bsz * t
    depth = w_in.shape[0]
    xs = x.reshape(n, d).astype(F32)
    y2 = wts = None
    for l in range(depth):
        w_main, w_gates = _split_w_in(w_in[l])
        if y2 is None:
            h, gates = _combine_norm(xs, None, None, attn_norm_w[l], w_gates, final=False)
        else:
            xs, h, gates = _combine_norm(xs, y2, wts, attn_norm_w[l], w_gates, final=False)
        proj = _inproj(h, w_main)
        ya = _gdn(proj, gates, gdn_conv_w[l], gdn_a_log[l], gdn_dt_bias[l], gdn_norm_w[l], bsz=bsz, t=t)
        ybc = _hgrn_sc(proj, hgrn_lower_bounds, hgrn_norm_w[l], sc_conv_w[l], bsz=bsz, t=t, layer=l)
        wr_hi, wr_lo, b_r = _router_weights(w_router_group[l], b_router_group[l], w_router_expert[l], b_router_expert[l])
        xs, eid, wts = _outproj_router(ya, ybc, xs, w_out[l].astype(BF16), ffn_norm_w[l], wr_hi, wr_lo, b_r)
        plan = _moe_plan(eid[:, :TOP_K], n)
        y2 = _moe_ffn(xs, plan, ffn_norm_w[l], w_gate, w_up, w_down, layer=l)
    (out,) = _combine_norm(xs, y2, wts, final_norm_w, None, final=True)
    return out.reshape(bsz, t, d).astype(x.dtype)
```

```python
import functools

import jax
import jax.numpy as jnp
from jax import lax
from jax.experimental import pallas as pl
from jax.experimental.pallas import tpu as pltpu

F32 = jnp.float32
BF16 = jnp.bfloat16
EPS = 1e-6

GDN_H, GDN_D, GDN_C, GDN_K = 8, 128, 64, 4
GDN_W = GDN_H * GDN_D
HG_H, HG_D, HG_C = 4, 128, 16
HG_W = HG_H * HG_D
HG_UNROLL = 4
SC_W, SC_K = 512, 3
N_GROUPS, EPG, N_EXP, D_FF, TOP_K = 8, 8, 64, 512, 2
MOE_BLK = 256
N_GATE_COLS = 2 * GDN_H

LANES = 128
SUBLANES = 8
VMEM_LIMIT = 56 * 1024 * 1024

PROJ_W = 4 * GDN_W + 2 * HG_W + 2 * HG_W + 3 * SC_W
OFF_QKV, OFF_Z = 0, 3 * GDN_W
OFF_HQ = 4 * GDN_W
OFF_HF, OFF_HI, OFF_HG = OFF_HQ + HG_W, OFF_HQ + 2 * HG_W, OFF_HQ + 3 * HG_W
OFF_SB = OFF_HQ + 4 * HG_W
OFF_SC, OFF_SH = OFF_SB + SC_W, OFF_SB + 2 * SC_W


def _dot(a, b):
    return jnp.dot(a, b, preferred_element_type=F32)


def _dot_nt(a, b):
    return lax.dot_general(a, b, (((1,), (1,)), ((), ())), preferred_element_type=F32)


def _dot_tn(a, b):
    return lax.dot_general(a, b, (((0,), (0,)), ((), ())), preferred_element_type=F32)


def _split_hi_lo(x):
    hi = x.astype(BF16)
    lo = (x - hi.astype(F32)).astype(BF16)
    return hi, lo


def _dot_exact_lhs(mask_bf16, x):
    hi, lo = _split_hi_lo(x)
    return _dot(mask_bf16, hi) + _dot(mask_bf16, lo)


def _sigmoid(x):
    return 1.0 / (1.0 + jnp.exp(-x))


def _silu(x):
    return x * _sigmoid(x)


def _softplus(x):
    return jnp.maximum(x, 0.0) + jnp.log1p(jnp.exp(-jnp.abs(x)))


def _rms_scale(x):
    return lax.rsqrt(jnp.mean(x * x, axis=-1, keepdims=True) + EPS)


def _combine_norm_kernel(*refs, n_add, write_x, final):
    it = iter(refs)
    x_ref = next(it)
    add_refs = [next(it) for _ in range(n_add)]
    xv = x_ref[...]
    if n_add:
        wts = next(it)[...]
        for k, a in enumerate(add_refs):
            xv = xv + a[...] * wts[:, k:k + 1]
    nw_ref = next(it)
    wg_ref = None if final else next(it)
    xo_ref = next(it) if write_x else None
    if write_x:
        xo_ref[...] = xv
    hn = xv * _rms_scale(xv) * nw_ref[...]
    if final:
        next(it)[...] = hn
    else:
        h_ref, g_ref = next(it), next(it)
        hb = hn.astype(BF16)
        h_ref[...] = hb
        g_ref[...] = _dot(hb, wg_ref[...])


def _combine_norm(x, y2, wts, norm_w, w_gates, *, final, tm=256):
    n, d = x.shape
    n_add = 0 if y2 is None else 2
    write_x = (n_add > 0) and not final
    row = lambda i: (i, 0)
    in_specs = [pl.BlockSpec((tm, d), row)]
    args = [x]
    if n_add:
        in_specs += [pl.BlockSpec((tm, d), row), pl.BlockSpec((tm, d), lambda i: (n // tm + i, 0)),
                     pl.BlockSpec((tm, LANES), row)]
        args += [y2, y2, wts]
    in_specs.append(pl.BlockSpec((1, d), lambda i: (0, 0)))
    args.append(norm_w.reshape(1, d))
    out_shape, out_specs = [], []
    if not final:
        in_specs.append(pl.BlockSpec((d, LANES), lambda i: (0, 0)))
        args.append(w_gates)
    if write_x:
        out_shape.append(jax.ShapeDtypeStruct((n, d), F32))
        out_specs.append(pl.BlockSpec((tm, d), row))
    if final:
        out_shape.append(jax.ShapeDtypeStruct((n, d), F32))
        out_specs.append(pl.BlockSpec((tm, d), row))
    else:
        out_shape += [jax.ShapeDtypeStruct((n, d), BF16), jax.ShapeDtypeStruct((n, LANES), F32)]
        out_specs += [pl.BlockSpec((tm, d), row), pl.BlockSpec((tm, LANES), row)]
    return pl.pallas_call(
        functools.partial(_combine_norm_kernel, n_add=n_add, write_x=write_x, final=final),
        grid=(n // tm,),
        in_specs=in_specs,
        out_specs=out_specs,
        out_shape=out_shape,
        compiler_params=pltpu.CompilerParams(dimension_semantics=("arbitrary",), vmem_limit_bytes=VMEM_LIMIT),
        name="combine_norm_final" if final else "combine_norm",
    )(*args)


def _inproj_kernel(h_ref, w_ref, o_ref):
    o_ref[...] = _dot(h_ref[...], w_ref[...]).astype(BF16)


def _inproj(h, w_main, *, tm=1024, tn=1280):
    n, d = h.shape
    pw = w_main.shape[1]
    return pl.pallas_call(
        _inproj_kernel,
        grid=(pw // tn, n // tm),
        in_specs=[pl.BlockSpec((tm, d), lambda j, i: (i, 0)), pl.BlockSpec((d, tn), lambda j, i: (0, j))],
        out_specs=pl.BlockSpec((tm, tn), lambda j, i: (i, j)),
        out_shape=jax.ShapeDtypeStruct((n, pw), BF16),
        compiler_params=pltpu.CompilerParams(dimension_semantics=("arbitrary", "arbitrary"), vmem_limit_bytes=VMEM_LIMIT),
        name="inproj",
    )(h, w_main)


def _unit_lower_inverse_minus_eye(lowers, eye, bd16, off32, off64):
    mm = lambda a, b: _dot(a.astype(BF16), b.astype(BF16))
    ld = [jnp.where(bd16, m, 0.0) for m in lowers]
    l2 = [mm(a, a) for a in ld]
    l4 = [mm(a, a) for a in l2]
    l8 = [mm(a, a) for a in l4]
    pa = [mm(eye - a, eye + b) for a, b in zip(ld, l2)]
    pb = [mm(eye + a, eye + b) for a, b in zip(l4, l8)]
    t16 = [mm(a, b) for a, b in zip(pa, pb)]
    x1 = [mm(t, jnp.where(off32, m, 0.0)) for t, m in zip(t16, lowers)]
    t32 = [t - mm(x, t) for t, x in zip(t16, x1)]
    x2 = [mm(t, jnp.where(off64, m, 0.0)) for t, m in zip(t32, lowers)]
    return [t - mm(x, t) - eye for t, x in zip(t32, x2)]


def _gdn_kernel(qkv_ref, z_ref, gat_ref, cw_ref, alog_ref, dtb_ref, nw_ref, o_ref,
                xs_ref, carry_ref, s_ref, *, tt):
    c, d, w = GDN_C, GDN_D, GDN_W

    @pl.when(pl.program_id(1) == 0)
    def _():
        carry_ref[...] = jnp.zeros_like(carry_ref)
        s_ref[...] = jnp.zeros_like(s_ref)

    x = qkv_ref[...].astype(F32)
    xp = jnp.concatenate([carry_ref[...], x], axis=0)
    base = SUBLANES - (GDN_K - 1)
    acc = cw_ref[0:1, :] * xp[base:base + tt]
    for k in range(1, GDN_K):
        acc = acc + cw_ref[k:k + 1, :] * xp[base + k:base + k + tt]
    carry_ref[...] = x[tt - SUBLANES:tt]
    xs_ref[...] = _silu(acc)

    row = lax.broadcasted_iota(jnp.int32, (c, c), 0)
    col = lax.broadcasted_iota(jnp.int32, (c, c), 1)
    causal = row >= col
    strict = row > col
    eye = (row == col).astype(F32)
    bd16 = (row // 16) == (col // 16)
    off32 = ((row // 32) == (col // 32)) & ((row // 16) != (col // 16))
    off64 = (row // 32) != (col // 32)
    tril = causal.astype(BF16)
    alog = alog_ref[...]
    dtb = dtb_ref[...]
    nw = nw_ref[...]

    def chunk(ci):
        rows = pl.ds(ci * c, c)
        gat = gat_ref[rows, :]
        beta_all = _sigmoid(gat)
        g_all = -jnp.exp(alog) * _softplus(gat + dtb)
        gc_all = _dot_exact_lhs(tril, g_all)
        gc_t = jnp.concatenate([gc_all, jnp.zeros((LANES - c, LANES), F32)], axis=0).T
        heads = range(GDN_H)
        q = [xs_ref[rows, h * d:(h + 1) * d] for h in heads]
        k = [xs_ref[rows, w + h * d:w + (h + 1) * d] for h in heads]
        v = [xs_ref[rows, 2 * w + h * d:2 * w + (h + 1) * d] for h in heads]
        qn = [a * lax.rsqrt(jnp.sum(a * a, axis=-1, keepdims=True) + EPS) * (d ** -0.5) for a in q]
        kn = [a * lax.rsqrt(jnp.sum(a * a, axis=-1, keepdims=True) + EPS) for a in k]
        beta = [beta_all[:, h:h + 1] for h in heads]
        gc = [gc_all[:, GDN_H + h:GDN_H + h + 1] for h in heads]
        gl = [a[c - 1:c, :] for a in gc]
        decay = [jnp.exp(jnp.where(causal, gc[h] - gc_t[GDN_H + h:GDN_H + h + 1, 0:c], -jnp.inf)) for h in heads]
        egc = [jnp.exp(a) for a in gc]
        kb = [a * b for a, b in zip(kn, beta)]
        knb = [a.astype(BF16) for a in kn]
        kk = [_dot_nt(a.astype(BF16), b) for a, b in zip(kb, knb)]
        qk = [_dot_nt(a.astype(BF16), b) for a, b in zip(qn, knb)]
        lower = [jnp.where(strict, a * b, 0.0) for a, b in zip(kk, decay)]
        attn = [(a * b).astype(BF16) for a, b in zip(qk, decay)]
        tm1 = _unit_lower_inverse_minus_eye(lower, eye, bd16, off32, off64)
        rhs = [jnp.concatenate([v[h] * beta[h], kb[h] * egc[h]], axis=1) for h in heads]
        sol = [r + _dot(t.astype(BF16), r.astype(BF16)) for t, r in zip(tm1, rhs)]
        qd = [(qn[h] * egc[h]).astype(BF16) for h in heads]
        kd = [(kn[h] * jnp.exp(gl[h] - gc[h])).astype(BF16) for h in heads]
        s = [s_ref[h] for h in heads]
        sb = [a.astype(BF16) for a in s]
        v_new = [sol[h][:, :d] - _dot(sol[h][:, d:].astype(BF16), sb[h]) for h in heads]
        vnb = [a.astype(BF16) for a in v_new]
        o = [_dot(qd[h], sb[h]) + _dot(attn[h], vnb[h]) for h in heads]
        s_new = [s[h] * jnp.exp(gl[h]) + _dot_tn(kd[h], vnb[h]) for h in heads]
        for h in heads:
            s_ref[h] = s_new[h]
        for h in heads:
            zz = z_ref[rows, h * d:(h + 1) * d].astype(F32)
            o_ref[rows, h * d:(h + 1) * d] = (o[h] * _rms_scale(o[h]) * nw * _silu(zz)).astype(BF16)

    for ci in range(tt // c):
        chunk(ci)


def _gdn(proj, gates, conv_w, a_log, dt_bias, norm_w, *, bsz, t, tt=256):
    n = proj.shape[0]
    nt = t // tt
    pad = LANES - N_GATE_COLS
    alog = jnp.pad(a_log.astype(F32), (GDN_H, pad)).reshape(1, LANES)
    dtb = jnp.pad(dt_bias.astype(F32), (GDN_H, pad)).reshape(1, LANES)
    rowmap = lambda b, i: (b * nt + i, 0)
    const = lambda b, i: (0, 0)
    return pl.pallas_call(
        functools.partial(_gdn_kernel, tt=tt),
        grid=(bsz, nt),
        in_specs=[
            pl.BlockSpec((tt, 3 * GDN_W), rowmap),
            pl.BlockSpec((tt, GDN_W), lambda b, i: (b * nt + i, OFF_Z // GDN_W)),
            pl.BlockSpec((tt, LANES), rowmap),
            pl.BlockSpec((GDN_K, 3 * GDN_W), const),
            pl.BlockSpec((1, LANES), const),
            pl.BlockSpec((1, LANES), const),
            pl.BlockSpec((1, GDN_D), const),
        ],
        out_specs=pl.BlockSpec((tt, GDN_W), rowmap),
        out_shape=jax.ShapeDtypeStruct((n, GDN_W), BF16),
        scratch_shapes=[
            pltpu.VMEM((tt, 3 * GDN_W), F32),
            pltpu.VMEM((SUBLANES, 3 * GDN_W), F32),
            pltpu.VMEM((GDN_H, GDN_D, GDN_D), F32),
        ],
        compiler_params=pltpu.CompilerParams(dimension_semantics=("arbitrary", "arbitrary"), vmem_limit_bytes=VMEM_LIMIT),
        name="gdn",
    )(proj, proj, gates, conv_w.astype(F32), alog, dtb, norm_w.astype(F32).reshape(1, GDN_D))


def _hgrn_sc_kernel(hq_ref, hf_ref, hi_ref, hg_ref, sb_ref, sc_ref, sh_ref, lbraw_ref, nw_ref, scw_ref,
                    o_ref, qs_ref, ks_ref, vs_ref, bs_ref, st_ref, carry_ref, *, tt, layer):
    c, d = HG_C, HG_D

    @pl.when(pl.program_id(1) == 0)
    def _():
        carry_ref[...] = jnp.zeros_like(carry_ref)
        st_ref[...] = jnp.zeros_like(st_ref)

    lbraw = lbraw_ref[...]
    e = jnp.exp(lbraw - jnp.max(lbraw, axis=0, keepdims=True))
    p = e / jnp.sum(e, axis=0, keepdims=True)
    lb = jnp.zeros((1, HG_W), F32)
    for kk in range(1, layer + 1):
        lb = lb + p[kk:kk + 1, :]

    fpre = hf_ref[...].astype(F32)
    log_sig = jnp.minimum(fpre, 0.0) - jnp.log1p(jnp.exp(-jnp.abs(fpre)))
    a = jnp.log(lb)
    cc = jnp.log1p(-lb) + log_sig
    log_f = jnp.maximum(a, cc) + jnp.log1p(jnp.exp(-jnp.abs(a - cc)))
    ri = lax.broadcasted_iota(jnp.int32, (tt, tt), 0)
    rj = lax.broadcasted_iota(jnp.int32, (tt, tt), 1)
    blocktril = (((ri // c) == (rj // c)) & (ri >= rj)).astype(BF16)
    bs_ref[...] = _dot_exact_lhs(blocktril, log_f)
    ks_ref[...] = (1.0 - lb) * _sigmoid(-fpre)
    qs_ref[...] = hq_ref[...].astype(F32)
    vs_ref[...] = hi_ref[...].astype(F32)

    prod = sc_ref[...].astype(F32) * sh_ref[...].astype(F32)
    xp = jnp.concatenate([carry_ref[...], prod], axis=0)
    base = SUBLANES - (SC_K - 1)
    conv = scw_ref[0:1, :] * xp[base:base + tt]
    for kk in range(1, SC_K):
        conv = conv + scw_ref[kk:kk + 1, :] * xp[base + kk:base + kk + tt]
    carry_ref[...] = prod[tt - SUBLANES:tt]
    o_ref[:, HG_W:HG_W + SC_W] = (sb_ref[...].astype(F32) * conv).astype(BF16)

    rowi = lax.broadcasted_iota(jnp.int32, (c, d), 0)
    ones = jnp.ones((d, d), BF16)
    nw = nw_ref[...]

    def chunk(ci, carry):
        r0 = pl.multiple_of(ci * c, c)
        rows = pl.ds(r0, c)
        heads = range(HG_H)
        cs = [slice(h * d, (h + 1) * d) for h in heads]
        q = [qs_ref[rows, s] for s in cs]
        k = [ks_ref[rows, s] for s in cs]
        v = [vs_ref[rows, s] for s in cs]
        b = [bs_ref[rows, s] for s in cs]
        bl = [a[c - 1:c, :] for a in b]
        st = [st_ref[h] for h in heads]
        o_inter = [_dot_nt((q[h] * jnp.exp(b[h])).astype(BF16), st[h].astype(BF16)) for h in heads]
        pmat = []
        for h in heads:
            parts = []
            for j in range(c):
                pj = q[h] * k[h][j:j + 1, :] * jnp.exp(jnp.minimum(b[h] - b[h][j:j + 1, :], 0.0))
                parts.append(jnp.where(rowi >= j, pj, 0.0))
            pmat.append(jnp.concatenate(parts, axis=0).astype(BF16))
        rsum = [_dot(p, ones) for p in pmat]
        o = []
        for h in heads:
            acc = o_inter[h] + rsum[h][0:c] * v[h][0:1, :]
            for j in range(1, c):
                acc = acc + rsum[h][j * c:(j + 1) * c] * v[h][j:j + 1, :]
            o.append(acc)
        kd = [(k[h] * jnp.exp(bl[h] - b[h])).astype(BF16) for h in heads]
        st_new = [st[h] * jnp.exp(bl[h]) + _dot_tn(v[h].astype(BF16), kd[h]) for h in heads]
        for h in heads:
            st_ref[h] = st_new[h]
        for h in heads:
            gate = hg_ref[rows, cs[h]].astype(F32)
            o_ref[rows, cs[h]] = (o[h] * _rms_scale(o[h]) * nw * _silu(gate)).astype(BF16)
        return carry

    lax.fori_loop(0, tt // c, chunk, 0, unroll=HG_UNROLL)


def _hgrn_sc(proj, lb_raw, norm_w, sc_conv_w, *, bsz, t, layer, tt=256):
    n = proj.shape[0]
    nt = t // tt
    depth = lb_raw.shape[0]
    slab = lambda off: pl.BlockSpec((tt, HG_W), lambda b, i: (b * nt + i, off // HG_W))
    const = lambda b, i: (0, 0)
    return pl.pallas_call(
        functools.partial(_hgrn_sc_kernel, tt=tt, layer=layer),
        grid=(bsz, nt),
        in_specs=[slab(OFF_HQ), slab(OFF_HF), slab(OFF_HI), slab(OFF_HG), slab(OFF_SB), slab(OFF_SC), slab(OFF_SH),
                  pl.BlockSpec((depth, HG_W), const),
                  pl.BlockSpec((1, HG_D), const),
                  pl.BlockSpec((SC_K, SC_W), const)],
        out_specs=pl.BlockSpec((tt, HG_W + SC_W), lambda b, i: (b * nt + i, 0)),
        out_shape=jax.ShapeDtypeStruct((n, HG_W + SC_W), BF16),
        scratch_shapes=[pltpu.VMEM((tt, HG_W), F32)] * 4 + [
            pltpu.VMEM((HG_H, HG_D, HG_D), F32),
            pltpu.VMEM((SUBLANES, SC_W), F32),
        ],
        compiler_params=pltpu.CompilerParams(dimension_semantics=("arbitrary", "arbitrary"), vmem_limit_bytes=VMEM_LIMIT),
        name="hgrn_sc",
    )(proj, proj, proj, proj, proj, proj, proj, lb_raw.astype(F32), norm_w.astype(F32).reshape(1, HG_D),
      sc_conv_w.astype(F32))


def _outproj_router_kernel(ya_ref, ybc_ref, x_ref, wo_ref, nw_ref, wrh_ref, wrl_ref, br_ref,
                           x2_ref, eid_ref, wts_ref):
    tm = x_ref.shape[0]
    x2 = x_ref[...] + _dot(ya_ref[...], wo_ref[0:GDN_W, :]) + _dot(ybc_ref[...], wo_ref[GDN_W:, :])
    x2_ref[...] = x2
    h = x2 * _rms_scale(x2) * nw_ref[...]
    hh, hl = _split_hi_lo(h)
    wrh, wrl = wrh_ref[...], wrl_ref[...]
    logits = _dot(hh, wrh) + (_dot(hh, wrl) + _dot(hl, wrh)) + br_ref[...]

    lane_i = lax.broadcasted_iota(jnp.int32, (tm, LANES), 1)
    lane = lane_i.astype(F32)
    lane_grp = ((lane_i - N_GROUPS) // EPG).astype(F32)
    is_g = lane_i < N_GROUPS
    is_e = (lane_i >= N_GROUPS) & (lane_i < N_GROUPS + N_EXP)
    big = float(4 * LANES)

    lg = jnp.where(is_g, logits, -jnp.inf)
    mg = jnp.max(lg, axis=-1, keepdims=True)
    p_group = 1.0 / jnp.sum(jnp.exp(lg - mg), axis=-1, keepdims=True)
    gidx = jnp.min(jnp.where(lg == mg, lane, big), axis=-1, keepdims=True)

    le = jnp.where(is_e & (lane_grp == gidx), logits, -jnp.inf)
    m1 = jnp.max(le, axis=-1, keepdims=True)
    i1 = jnp.min(jnp.where(le == m1, lane, big), axis=-1, keepdims=True)
    le2 = jnp.where(lane == i1, -jnp.inf, le)
    m2 = jnp.max(le2, axis=-1, keepdims=True)
    i2 = jnp.min(jnp.where(le2 == m2, lane, big), axis=-1, keepdims=True)
    e2 = jnp.exp(m2 - m1)
    w1 = p_group / (1.0 + e2)
    w2 = p_group * e2 / (1.0 + e2)
    eid = jnp.where(lane_i == 0, i1 - N_GROUPS, jnp.where(lane_i == 1, i2 - N_GROUPS, 0.0))
    eid_ref[...] = eid.astype(jnp.int32)
    wts_ref[...] = jnp.where(lane_i == 0, w1, jnp.where(lane_i == 1, w2, 0.0))


def _outproj_router(ya, ybc, x, w_out, norm_w, wr_hi, wr_lo, b_r, *, tm=256):
    n, d = x.shape
    row = lambda i: (i, 0)
    const = lambda i: (0, 0)
    return pl.pallas_call(
        _outproj_router_kernel,
        grid=(n // tm,),
        in_specs=[pl.BlockSpec((tm, GDN_W), row), pl.BlockSpec((tm, HG_W + SC_W), row), pl.BlockSpec((tm, d), row),
                  pl.BlockSpec((d, d), const), pl.BlockSpec((1, d), const),
                  pl.BlockSpec((d, LANES), const), pl.BlockSpec((d, LANES), const), pl.BlockSpec((1, LANES), const)],
        out_specs=[pl.BlockSpec((tm, d), row), pl.BlockSpec((tm, LANES), row), pl.BlockSpec((tm, LANES), row)],
        out_shape=[jax.ShapeDtypeStruct((n, d), F32), jax.ShapeDtypeStruct((n, LANES), jnp.int32),
                   jax.ShapeDtypeStruct((n, LANES), F32)],
        compiler_params=pltpu.CompilerParams(dimension_semantics=("arbitrary",), vmem_limit_bytes=VMEM_LIMIT),
        name="outproj_router",
    )(ya, ybc, x, w_out, norm_w.reshape(1, d), wr_hi, wr_lo, b_r)


def _moe_plan(eid, n_tok):
    blk = MOE_BLK
    n_assign = n_tok * TOP_K
    n_blocks = -(-n_assign // blk) + N_EXP
    flat_e = eid.reshape(-1).astype(jnp.int32)
    shift = max(1, (n_assign - 1).bit_length())
    keys = lax.sort(flat_e * (1 << shift) + jnp.arange(n_assign, dtype=jnp.int32))
    order = keys & ((1 << shift) - 1)
    experts = jnp.arange(N_EXP, dtype=jnp.int32)
    counts = jnp.sum((flat_e[:, None] == experts[None, :]).astype(jnp.int32), axis=0)
    starts = jnp.cumsum(counts) - counts
    padded = (counts + blk - 1) // blk * blk
    pad_end = jnp.cumsum(padded)
    pad_start = pad_end - padded
    n_active = (pad_end[-1] // blk).astype(jnp.int32)
    bidx = jnp.arange(n_blocks, dtype=jnp.int32)
    bstart = bidx * blk
    e_b = jnp.minimum(jnp.sum((bstart[:, None] >= pad_end[None, :]).astype(jnp.int32), axis=1), N_EXP - 1)
    j0 = bstart - pad_start[e_b]
    nvalid = jnp.clip(counts[e_b] - j0, 0, blk)
    i = jnp.arange(blk, dtype=jnp.int32)[None, :]
    valid = i < nvalid[:, None]
    src = jnp.clip((starts[e_b] + j0)[:, None] + i, 0, n_assign - 1)
    a = order[src]
    tok = a // TOP_K
    slot = a % TOP_K
    row_tok = jnp.where(valid, tok, 0).astype(jnp.int32)
    trash = TOP_K * n_tok + (bidx % 2)[:, None] * blk + i
    row_dst = jnp.where(valid, slot * n_tok + tok, trash).astype(jnp.int32)
    block_expert = e_b[jnp.minimum(bidx, jnp.maximum(n_active - 1, 0))]
    return block_expert, n_active.reshape(1), row_tok[:, None, :], row_dst[:, None, :]


def _moe_kernel(bexp_ref, nact_ref, x_hbm, tokc_ref, tokn_ref, dst_ref, nw_ref, wg_ref, wu_ref, wd_ref,
                y_hbm, xbuf, ybuf, wgb, wub, wdb, gsem, ssem):
    blk = MOE_BLK
    b = pl.program_id(0)
    nact = nact_ref[0]
    slot = lax.rem(b, 2)
    other = 1 - slot
    active = b < nact

    def gather_start(tok_ref, s):
        for r in range(blk):
            pltpu.make_async_copy(x_hbm.at[pl.ds(tok_ref[0, r], 1), :], xbuf.at[s, pl.ds(r, 1), :], gsem.at[s]).start()

    def gather_wait(s):
        for r in range(blk):
            pltpu.make_async_copy(x_hbm.at[pl.ds(0, 1), :], xbuf.at[s, pl.ds(r, 1), :], gsem.at[s]).wait()

    def scatter_start(s):
        for r in range(blk):
            pltpu.make_async_copy(ybuf.at[s, pl.ds(r, 1), :], y_hbm.at[pl.ds(dst_ref[0, r], 1), :], ssem.at[s]).start()

    def scatter_wait(s):
        for r in range(blk):
            pltpu.make_async_copy(ybuf.at[s, pl.ds(r, 1), :], y_hbm.at[pl.ds(0, 1), :], ssem.at[s]).wait()

    @pl.when(b == 0)
    def _():
        gather_start(tokc_ref, 0)

    @pl.when(active & (b >= 2))
    def _():
        scatter_wait(slot)

    @pl.when(active & ((b == 0) | (bexp_ref[b] != bexp_ref[jnp.maximum(b - 1, 0)])))
    def _():
        wgb[...] = wg_ref[...].astype(BF16)
        wub[...] = wu_ref[...].astype(BF16)
        wdb[...] = wd_ref[...].astype(BF16)

    @pl.when(active)
    def _():
        gather_start(tokn_ref, other)
        gather_wait(slot)
        xv = xbuf[slot]
        hb = (xv * _rms_scale(xv) * nw_ref[...]).astype(BF16)
        g = _dot(hb, wgb[...])
        u = _dot(hb, wub[...])
        hid = (_silu(g) * u).astype(BF16)
        ybuf[slot] = _dot(hid, wdb[...])
        scatter_start(slot)

    @pl.when(b == nact - 1)
    def _():
        gather_wait(other)
        scatter_wait(slot)

        @pl.when(b >= 1)
        def _():
            scatter_wait(other)


def _moe_ffn(x2, plan, norm_w, w_gate, w_up, w_down, *, layer):
    n, d = x2.shape
    blk = MOE_BLK
    block_expert, n_active, row_tok, row_dst = plan
    n_blocks = row_tok.shape[0]
    wmap = lambda b, bexp, nact: (layer, bexp[b], 0, 0)
    grid_spec = pltpu.PrefetchScalarGridSpec(
        num_scalar_prefetch=2,
        grid=(n_blocks,),
        in_specs=[
            pl.BlockSpec(memory_space=pl.ANY),
            pl.BlockSpec((None, 1, blk), lambda b, bexp, nact: (b, 0, 0), memory_space=pltpu.SMEM),
            pl.BlockSpec((None, 1, blk), lambda b, bexp, nact: (jnp.minimum(b + 1, n_blocks - 1), 0, 0),
                         memory_space=pltpu.SMEM),
            pl.BlockSpec((None, 1, blk), lambda b, bexp, nact: (b, 0, 0), memory_space=pltpu.SMEM),
            pl.BlockSpec((1, d), lambda b, bexp, nact: (0, 0)),
            pl.BlockSpec((None, None, d, D_FF), wmap),
            pl.BlockSpec((None, None, d, D_FF), wmap),
            pl.BlockSpec((None, None, D_FF, d), wmap),
        ],
        out_specs=pl.BlockSpec(memory_space=pl.ANY),
        scratch_shapes=[pltpu.VMEM((2, blk, d), F32), pltpu.VMEM((2, blk, d), F32),
                        pltpu.VMEM((d, D_FF), BF16), pltpu.VMEM((d, D_FF), BF16), pltpu.VMEM((D_FF, d), BF16),
                        pltpu.SemaphoreType.DMA((2,)), pltpu.SemaphoreType.DMA((2,))],
    )
    return pl.pallas_call(
        _moe_kernel,
        grid_spec=grid_spec,
        out_shape=jax.ShapeDtypeStruct((TOP_K * n + 2 * blk, d), F32),
        compiler_params=pltpu.CompilerParams(dimension_semantics=("arbitrary",), vmem_limit_bytes=VMEM_LIMIT),
        name="moe_ffn",
    )(block_expert, n_active, x2, row_tok, row_tok, row_dst, norm_w.reshape(1, d), w_gate, w_up, w_down)


def _split_w_in(w_in_l):
    g0 = 4 * GDN_W
    w_main = jnp.concatenate([w_in_l[:, :g0], w_in_l[:, g0 + N_GATE_COLS:]], axis=1).astype(BF16)
    w_gates = jnp.pad(w_in_l[:, g0:g0 + N_GATE_COLS], ((0, 0), (0, LANES - N_GATE_COLS))).astype(BF16)
    return w_main, w_gates


def _router_weights(w_rg, b_rg, w_re, b_re):
    d = w_rg.shape[0]
    pad = LANES - N_GROUPS - N_EXP
    w_r = jnp.concatenate([w_rg, w_re, jnp.zeros((d, pad), F32)], axis=1).astype(F32)
    b_r = jnp.concatenate([b_rg, b_re, jnp.zeros((pad,), F32)]).astype(F32).reshape(1, LANES)
    hi, lo = _split_hi_lo(w_r)
    return hi, lo, b_r


def kernel(x, attn_norm_w, w_in, gdn_conv_w, gdn_a_log, gdn_dt_bias, gdn_norm_w, hgrn_lower_bounds, hgrn_norm_w, sc_conv_w, w_out, ffn_norm_w, w_router_group, b_router_group, w_router_expert, b_router_expert, w_gate, w_up, w_down, final_norm_w):
    bsz, t, d = x.shape
    n = bsz * t
    depth = w_in.shape[0]
    xs = x.reshape(n, d).astype(F32)
    y2 = wts = None
    for l in range(depth):
        w_main, w_gates = _split_w_in(w_in[l])
        if y2 is None:
            h, gates = _combine_norm(xs, None, None, attn_norm_w[l], w_gates, final=False)
        else:
            xs, h, gates = _combine_norm(xs, y2, wts, attn_norm_w[l], w_gates, final=False)
        proj = _inproj(h, w_main)
        ya = _gdn(proj, gates, gdn_conv_w[l], gdn_a_log[l], gdn_dt_bias[l], gdn_norm_w[l], bsz=bsz, t=t)
        ybc = _hgrn_sc(proj, hgrn_lower_bounds, hgrn_norm_w[l], sc_conv_w[l], bsz=bsz, t=t, layer=l)
        wr_hi, wr_lo, b_r = _router_weights(w_router_group[l], b_router_group[l], w_router_expert[l], b_router_expert[l])
        xs, eid, wts = _outproj_router(ya, ybc, xs, w_out[l].astype(BF16), ffn_norm_w[l], wr_hi, wr_lo, b_r)
        plan = _moe_plan(eid[:, :TOP_K], n)
        y2 = _moe_ffn(xs, plan, ffn_norm_w[l], w_gate, w_up, w_down, layer=l)
    (out,) = _combine_norm(xs, y2, wts, final_norm_w, None, final=True)
    return out.reshape(bsz, t, d).astype(x.dtype)
```

```python
import functools

import jax
import jax.numpy as jnp
from jax import lax
from jax.experimental import pallas as pl
from jax.experimental.pallas import tpu as pltpu

F32 = jnp.float32
BF16 = jnp.bfloat16
EPS = 1e-6

GDN_H, GDN_D, GDN_C, GDN_K = 8, 128, 64, 4
GDN_W = GDN_H * GDN_D
HG_H, HG_D, HG_C = 4, 128, 16
HG_W = HG_H * HG_D
HG_UNROLL = 4
SC_W, SC_K = 512, 3
N_GROUPS, EPG, N_EXP, D_FF, TOP_K = 8, 8, 64, 512, 2
MOE_BLK = 256
N_GATE_COLS = 2 * GDN_H

LANES = 128
SUBLANES = 8
VMEM_LIMIT = 56 * 1024 * 1024

PROJ_W = 4 * GDN_W + 2 * HG_W + 2 * HG_W + 3 * SC_W
OFF_QKV, OFF_Z = 0, 3 * GDN_W
OFF_HQ = 4 * GDN_W
OFF_HF, OFF_HI, OFF_HG = OFF_HQ + HG_W, OFF_HQ + 2 * HG_W, OFF_HQ + 3 * HG_W
OFF_SB = OFF_HQ + 4 * HG_W
OFF_SC, OFF_SH = OFF_SB + SC_W, OFF_SB + 2 * SC_W


def _dot(a, b):
    return jnp.dot(a, b, preferred_element_type=F32)


def _dot_nt(a, b):
    return lax.dot_general(a, b, (((1,), (1,)), ((), ())), preferred_element_type=F32)


def _dot_tn(a, b):
    return lax.dot_general(a, b, (((0,), (0,)), ((), ())), preferred_element_type=F32)


def _split_hi_lo(x):
    hi = x.astype(BF16)
    lo = (x - hi.astype(F32)).astype(BF16)
    return hi, lo


def _dot_exact_lhs(mask_bf16, x):
    hi, lo = _split_hi_lo(x)
    return _dot(mask_bf16, hi) + _dot(mask_bf16, lo)


def _sigmoid(x):
    return 1.0 / (1.0 + jnp.exp(-x))


def _silu(x):
    return x * _sigmoid(x)


def _softplus(x):
    return jnp.maximum(x, 0.0) + jnp.log1p(jnp.exp(-jnp.abs(x)))


def _rms_scale(x):
    return lax.rsqrt(jnp.mean(x * x, axis=-1, keepdims=True) + EPS)


def _combine_norm_kernel(*refs, n_add, write_x, final):
    it = iter(refs)
    x_ref = next(it)
    add_refs = [next(it) for _ in range(n_add)]
    xv = x_ref[...]
    if n_add:
        wts = next(it)[...]
        for k, a in enumerate(add_refs):
            xv = xv + a[...] * wts[:, k:k + 1]
    nw_ref = next(it)
    wg_ref = None if final else next(it)
    xo_ref = next(it) if write_x else None
    if write_x:
        xo_ref[...] = xv
    hn = xv * _rms_scale(xv) * nw_ref[...]
    if final:
        next(it)[...] = hn
    else:
        h_ref, g_ref = next(it), next(it)
        hb = hn.astype(BF16)
        h_ref[...] = hb
        g_ref[...] = _dot(hb, wg_ref[...])


def _combine_norm(x, y2, wts, norm_w, w_gates, *, final, tm=256):
    n, d = x.shape
    n_add = 0 if y2 is None else 2
    write_x = (n_add > 0) and not final
    row = lambda i: (i, 0)
    in_specs = [pl.BlockSpec((tm, d), row)]
    args = [x]
    if n_add:
        in_specs += [pl.BlockSpec((tm, d), row), pl.BlockSpec((tm, d), lambda i: (n // tm + i, 0)),
                     pl.BlockSpec((tm, LANES), row)]
        args += [y2, y2, wts]
    in_specs.append(pl.BlockSpec((1, d), lambda i: (0, 0)))
    args.append(norm_w.reshape(1, d))
    out_shape, out_specs = [], []
    if not final:
        in_specs.append(pl.BlockSpec((d, LANES), lambda i: (0, 0)))
        args.append(w_gates)
    if write_x:
        out_shape.append(jax.ShapeDtypeStruct((n, d), F32))
        out_specs.append(pl.BlockSpec((tm, d), row))
    if final:
        out_shape.append(jax.ShapeDtypeStruct((n, d), F32))
        out_specs.append(pl.BlockSpec((tm, d), row))
    else:
        out_shape += [jax.ShapeDtypeStruct((n, d), BF16), jax.ShapeDtypeStruct((n, LANES), F32)]
        out_specs += [pl.BlockSpec((tm, d), row), pl.BlockSpec((tm, LANES), row)]
    return pl.pallas_call(
        functools.partial(_combine_norm_kernel, n_add=n_add, write_x=write_x, final=final),
        grid=(n // tm,),
        in_specs=in_specs,
        out_specs=out_specs,
        out_shape=out_shape,
        compiler_params=pltpu.CompilerParams(dimension_semantics=("arbitrary",), vmem_limit_bytes=VMEM_LIMIT),
        name="combine_norm_final" if final else "combine_norm",
    )(*args)


def _inproj_kernel(h_ref, w_ref, o_ref):
    o_ref[...] = _dot(h_ref[...], w_ref[...]).astype(BF16)


def _inproj(h, w_main, *, tm=1024, tn=1280):
    n, d = h.shape
    pw = w_main.shape[1]
    return pl.pallas_call(
        _inproj_kernel,
        grid=(pw // tn, n // tm),
        in_specs=[pl.BlockSpec((tm, d), lambda j, i: (i, 0)), pl.BlockSpec((d, tn), lambda j, i: (0, j))],
        out_specs=pl.BlockSpec((tm, tn), lambda j, i: (i, j)),
        out_shape=jax.ShapeDtypeStruct((n, pw), BF16),
        compiler_params=pltpu.CompilerParams(dimension_semantics=("arbitrary", "arbitrary"), vmem_limit_bytes=VMEM_LIMIT),
        name="inproj",
    )(h, w_main)


def _unit_lower_inverse_minus_eye(lowers, eye, bd16, off32, off64):
    mm = lambda a, b: _dot(a.astype(BF16), b.astype(BF16))
    ld = [jnp.where(bd16, m, 0.0) for m in lowers]
    l2 = [mm(a, a) for a in ld]
    l4 = [mm(a, a) for a in l2]
    l8 = [mm(a, a) for a in l4]
    pa = [mm(eye - a, eye + b) for a, b in zip(ld, l2)]
    pb = [mm(eye + a, eye + b) for a, b in zip(l4, l8)]
    t16 = [mm(a, b) for a, b in zip(pa, pb)]
    x1 = [mm(t, jnp.where(off32, m, 0.0)) for t, m in zip(t16, lowers)]
    t32 = [t - mm(x, t) for t, x in zip(t16, x1)]
    x2 = [mm(t, jnp.where(off64, m, 0.0)) for t, m in zip(t32, lowers)]
    return [t - mm(x, t) - eye for t, x in zip(t32, x2)]


def _gdn_kernel(qkv_ref, z_ref, gat_ref, cw_ref, alog_ref, dtb_ref, nw_ref, o_ref,
                xs_ref, carry_ref, s_ref, *, tt):
    c, d, w = GDN_C, GDN_D, GDN_W

    @pl.when(pl.program_id(1) == 0)
    def _():
        carry_ref[...] = jnp.zeros_like(carry_ref)
        s_ref[...] = jnp.zeros_like(s_ref)

    x = qkv_ref[...].astype(F32)
    xp = jnp.concatenate([carry_ref[...], x], axis=0)
    base = SUBLANES - (GDN_K - 1)
    acc = cw_ref[0:1, :] * xp[base:base + tt]
    for k in range(1, GDN_K):
        acc = acc + cw_ref[k:k + 1, :] * xp[base + k:base + k + tt]
    carry_ref[...] = x[tt - SUBLANES:tt]
    xs_ref[...] = _silu(acc)

    row = lax.broadcasted_iota(jnp.int32, (c, c), 0)
    col = lax.broadcasted_iota(jnp.int32, (c, c), 1)
    causal = row >= col
    strict = row > col
    eye = (row == col).astype(F32)
    bd16 = (row // 16) == (col // 16)
    off32 = ((row // 32) == (col // 32)) & ((row // 16) != (col // 16))
    off64 = (row // 32) != (col // 32)
    tril = causal.astype(BF16)
    alog = alog_ref[...]
    dtb = dtb_ref[...]
    nw = nw_ref[...]

    def chunk(ci):
        rows = pl.ds(ci * c, c)
        gat = gat_ref[rows, :]
        beta_all = _sigmoid(gat)
        g_all = -jnp.exp(alog) * _softplus(gat + dtb)
        gc_all = _dot_exact_lhs(tril, g_all)
        gc_t = jnp.concatenate([gc_all, jnp.zeros((LANES - c, LANES), F32)], axis=0).T
        heads = range(GDN_H)
        q = [xs_ref[rows, h * d:(h + 1) * d] for h in heads]
        k = [xs_ref[rows, w + h * d:w + (h + 1) * d] for h in heads]
        v = [xs_ref[rows, 2 * w + h * d:2 * w + (h + 1) * d] for h in heads]
        qn = [a * lax.rsqrt(jnp.sum(a * a, axis=-1, keepdims=True) + EPS) * (d ** -0.5) for a in q]
        kn = [a * lax.rsqrt(jnp.sum(a * a, axis=-1, keepdims=True) + EPS) for a in k]
        beta = [beta_all[:, h:h + 1] for h in heads]
        gc = [gc_all[:, GDN_H + h:GDN_H + h + 1] for h in heads]
        gl = [a[c - 1:c, :] for a in gc]
        decay = [jnp.exp(jnp.where(causal, gc[h] - gc_t[GDN_H + h:GDN_H + h + 1, 0:c], -jnp.inf)) for h in heads]
        egc = [jnp.exp(a) for a in gc]
        kb = [a * b for a, b in zip(kn, beta)]
        knb = [a.astype(BF16) for a in kn]
        kk = [_dot_nt(a.astype(BF16), b) for a, b in zip(kb, knb)]
        qk = [_dot_nt(a.astype(BF16), b) for a, b in zip(qn, knb)]
        lower = [jnp.where(strict, a * b, 0.0) for a, b in zip(kk, decay)]
        attn = [(a * b).astype(BF16) for a, b in zip(qk, decay)]
        tm1 = _unit_lower_inverse_minus_eye(lower, eye, bd16, off32, off64)
        rhs = [jnp.concatenate([v[h] * beta[h], kb[h] * egc[h]], axis=1) for h in heads]
        sol = [r + _dot(t.astype(BF16), r.astype(BF16)) for t, r in zip(tm1, rhs)]
        qd = [(qn[h] * egc[h]).astype(BF16) for h in heads]
        kd = [(kn[h] * jnp.exp(gl[h] - gc[h])).astype(BF16) for h in heads]
        s = [s_ref[h] for h in heads]
        sb = [a.astype(BF16) for a in s]
        v_new = [sol[h][:, :d] - _dot(sol[h][:, d:].astype(BF16), sb[h]) for h in heads]
        vnb = [a.astype(BF16) for a in v_new]
        o = [_dot(qd[h], sb[h]) + _dot(attn[h], vnb[h]) for h in heads]
        s_new = [s[h] * jnp.exp(gl[h]) + _dot_tn(kd[h], vnb[h]) for h in heads]
        for h in heads:
            s_ref[h] = s_new[h]
        for h in heads:
            zz = z_ref[rows, h * d:(h + 1) * d].astype(F32)
            o_ref[rows, h * d:(h + 1) * d] = (o[h] * _rms_scale(o[h]) * nw * _silu(zz)).astype(BF16)

    for ci in range(tt // c):
        chunk(ci)


def _gdn(proj, gates, conv_w, a_log, dt_bias, norm_w, *, bsz, t, tt=256):
    n = proj.shape[0]
    nt = t // tt
    pad = LANES - N_GATE_COLS
    alog = jnp.pad(a_log.astype(F32), (GDN_H, pad)).reshape(1, LANES)
    dtb = jnp.pad(dt_bias.astype(F32), (GDN_H, pad)).reshape(1, LANES)
    rowmap = lambda b, i: (b * nt + i, 0)
    const = lambda b, i: (0, 0)
    return pl.pallas_call(
        functools.partial(_gdn_kernel, tt=tt),
        grid=(bsz, nt),
        in_specs=[
            pl.BlockSpec((tt, 3 * GDN_W), rowmap),
            pl.BlockSpec((tt, GDN_W), lambda b, i: (b * nt + i, OFF_Z // GDN_W)),
            pl.BlockSpec((tt, LANES), rowmap),
            pl.BlockSpec((GDN_K, 3 * GDN_W), const),
            pl.BlockSpec((1, LANES), const),
            pl.BlockSpec((1, LANES), const),
            pl.BlockSpec((1, GDN_D), const),
        ],
        out_specs=pl.BlockSpec((tt, GDN_W), rowmap),
        out_shape=jax.ShapeDtypeStruct((n, GDN_W), BF16),
        scratch_shapes=[
            pltpu.VMEM((tt, 3 * GDN_W), F32),
            pltpu.VMEM((SUBLANES, 3 * GDN_W), F32),
            pltpu.VMEM((GDN_H, GDN_D, GDN_D), F32),
        ],
        compiler_params=pltpu.CompilerParams(dimension_semantics=("arbitrary", "arbitrary"), vmem_limit_bytes=VMEM_LIMIT),
        name="gdn",
    )(proj, proj, gates, conv_w.astype(F32), alog, dtb, norm_w.astype(F32).reshape(1, GDN_D))


def _hgrn_sc_kernel(hq_ref, hf_ref, hi_ref, hg_ref, sb_ref, sc_ref, sh_ref, lbraw_ref, nw_ref, scw_ref,
                    o_ref, qs_ref, ks_ref, vs_ref, bs_ref, st_ref, carry_ref, *, tt, layer):
    c, d = HG_C, HG_D

    @pl.when(pl.program_id(1) == 0)
    def _():
        carry_ref[...] = jnp.zeros_like(carry_ref)
        st_ref[...] = jnp.zeros_like(st_ref)

    lbraw = lbraw_ref[...]
    e = jnp.exp(lbraw - jnp.max(lbraw, axis=0, keepdims=True))
    p = e / jnp.sum(e, axis=0, keepdims=True)
    lb = jnp.zeros((1, HG_W), F32)
    for kk in range(1, layer + 1):
        lb = lb + p[kk:kk + 1, :]

    fpre = hf_ref[...].astype(F32)
    log_sig = jnp.minimum(fpre, 0.0) - jnp.log1p(jnp.exp(-jnp.abs(fpre)))
    a = jnp.log(lb)
    cc = jnp.log1p(-lb) + log_sig
    log_f = jnp.maximum(a, cc) + jnp.log1p(jnp.exp(-jnp.abs(a - cc)))
    ri = lax.broadcasted_iota(jnp.int32, (tt, tt), 0)
    rj = lax.broadcasted_iota(jnp.int32, (tt, tt), 1)
    blocktril = (((ri // c) == (rj // c)) & (ri >= rj)).astype(BF16)
    bs_ref[...] = _dot_exact_lhs(blocktril, log_f)
    ks_ref[...] = (1.0 - lb) * _sigmoid(-fpre)
    qs_ref[...] = hq_ref[...].astype(F32)
    vs_ref[...] = hi_ref[...].astype(F32)

    prod = sc_ref[...].astype(F32) * sh_ref[...].astype(F32)
    xp = jnp.concatenate([carry_ref[...], prod], axis=0)
    base = SUBLANES - (SC_K - 1)
    conv = scw_ref[0:1, :] * xp[base:base + tt]
    for kk in range(1, SC_K):
        conv = conv + scw_ref[kk:kk + 1, :] * xp[base + kk:base + kk + tt]
    carry_ref[...] = prod[tt - SUBLANES:tt]
    o_ref[:, HG_W:HG_W + SC_W] = (sb_ref[...].astype(F32) * conv).astype(BF16)

    rowi = lax.broadcasted_iota(jnp.int32, (c, d), 0)
    ones = jnp.ones((d, d), BF16)
    nw = nw_ref[...]

    def chunk(ci, carry):
        r0 = pl.multiple_of(ci * c, c)
        rows = pl.ds(r0, c)
        heads = range(HG_H)
        cs = [slice(h * d, (h + 1) * d) for h in heads]
        q = [qs_ref[rows, s] for s in cs]
        k = [ks_ref[rows, s] for s in cs]
        v = [vs_ref[rows, s] for s in cs]
        b = [bs_ref[rows, s] for s in cs]
        bl = [a[c - 1:c, :] for a in b]
        st = [st_ref[h] for h in heads]
        o_inter = [_dot_nt((q[h] * jnp.exp(b[h])).astype(BF16), st[h].astype(BF16)) for h in heads]
        pmat = []
        for h in heads:
            parts = []
            for j in range(c):
                pj = q[h] * k[h][j:j + 1, :] * jnp.exp(jnp.minimum(b[h] - b[h][j:j + 1, :], 0.0))
                parts.append(jnp.where(rowi >= j, pj, 0.0))
            pmat.append(jnp.concatenate(parts, axis=0).astype(BF16))
        rsum = [_dot(p, ones) for p in pmat]
        o = []
        for h in heads:
            acc = o_inter[h] + rsum[h][0:c] * v[h][0:1, :]
            for j in range(1, c):
                acc = acc + rsum[h][j * c:(j + 1) * c] * v[h][j:j + 1, :]
            o.append(acc)
        kd = [(k[h] * jnp.exp(bl[h] - b[h])).astype(BF16) for h in heads]
        st_new = [st[h] * jnp.exp(bl[h]) + _dot_tn(v[h].astype(BF16), kd[h]) for h in heads]
        for h in heads:
            st_ref[h] = st_new[h]
        for h in heads:
            gate = hg_ref[rows, cs[h]].astype(F32)
            o_ref[rows, cs[h]] = (o[h] * _rms_scale(o[h]) * nw * _silu(gate)).astype(BF16)
        return carry

    lax.fori_loop(0, tt // c, chunk, 0, unroll=HG_UNROLL)


def _hgrn_sc(proj, lb_raw, norm_w, sc_conv_w, *, bsz, t, layer, tt=256):
    n = proj.shape[0]
    nt = t // tt
    depth = lb_raw.shape[0]
    slab = lambda off: pl.BlockSpec((tt, HG_W), lambda b, i: (b * nt + i, off // HG_W))
    const = lambda b, i: (0, 0)
    return pl.pallas_call(
        functools.partial(_hgrn_sc_kernel, tt=tt, layer=layer),
        grid=(bsz, nt),
        in_specs=[slab(OFF_HQ), slab(OFF_HF), slab(OFF_HI), slab(OFF_HG), slab(OFF_SB), slab(OFF_SC), slab(OFF_SH),
                  pl.BlockSpec((depth, HG_W), const),
                  pl.BlockSpec((1, HG_D), const),
                  pl.BlockSpec((SC_K, SC_W), const)],
        out_specs=pl.BlockSpec((tt, HG_W + SC_W), lambda b, i: (b * nt + i, 0)),
        out_shape=jax.ShapeDtypeStruct((n, HG_W + SC_W), BF16),
        scratch_shapes=[pltpu.VMEM((tt, HG_W), F32)] * 4 + [
            pltpu.VMEM((HG_H, HG_D, HG_D), F32),
            pltpu.VMEM((SUBLANES, SC_W), F32),
        ],
        compiler_params=pltpu.CompilerParams(dimension_semantics=("arbitrary", "arbitrary"), vmem_limit_bytes=VMEM_LIMIT),
        name="hgrn_sc",
    )(proj, proj, proj, proj, proj, proj, proj, lb_raw.astype(F32), norm_w.astype(F32).reshape(1, HG_D),
      sc_conv_w.astype(F32))


def _outproj_router_kernel(ya_ref, ybc_ref, x_ref, wo_ref, nw_ref, wrh_ref, wrl_ref, br_ref,
                           x2_ref, eid_ref, wts_ref):
    tm = x_ref.shape[0]
    x2 = x_ref[...] + _dot(ya_ref[...], wo_ref[0:GDN_W, :]) + _dot(ybc_ref[...], wo_ref[GDN_W:, :])
    x2_ref[...] = x2
    h = x2 * _rms_scale(x2) * nw_ref[...]
    hh, hl = _split_hi_lo(h)
    wrh, wrl = wrh_ref[...], wrl_ref[...]
    logits = _dot(hh, wrh) + (_dot(hh, wrl) + _dot(hl, wrh)) + br_ref[...]

    lane_i = lax.broadcasted_iota(jnp.int32, (tm, LANES), 1)
    lane = lane_i.astype(F32)
    lane_grp = ((lane_i - N_GROUPS) // EPG).astype(F32)
    is_g = lane_i < N_GROUPS
    is_e = (lane_i >= N_GROUPS) & (lane_i < N_GROUPS + N_EXP)
    big = float(4 * LANES)

    lg = jnp.where(is_g, logits, -jnp.inf)
    mg = jnp.max(lg, axis=-1, keepdims=True)
    p_group = 1.0 / jnp.sum(jnp.exp(lg - mg), axis=-1, keepdims=True)
    gidx = jnp.min(jnp.where(lg == mg, lane, big), axis=-1, keepdims=True)

    le = jnp.where(is_e & (lane_grp == gidx), logits, -jnp.inf)
    m1 = jnp.max(le, axis=-1, keepdims=True)
    i1 = jnp.min(jnp.where(le == m1, lane, big), axis=-1, keepdims=True)
    le2 = jnp.where(lane == i1, -jnp.inf, le)
    m2 = jnp.max(le2, axis=-1, keepdims=True)
    i2 = jnp.min(jnp.where(le2 == m2, lane, big), axis=-1, keepdims=True)
    e2 = jnp.exp(m2 - m1)
    w1 = p_group / (1.0 + e2)
    w2 = p_group * e2 / (1.0 + e2)
    eid = jnp.where(lane_i == 0, i1 - N_GROUPS, jnp.where(lane_i == 1, i2 - N_GROUPS, 0.0))
    eid_ref[...] = eid.astype(jnp.int32)
    wts_ref[...] = jnp.where(lane_i == 0, w1, jnp.where(lane_i == 1, w2, 0.0))


def _outproj_router(ya, ybc, x, w_out, norm_w, wr_hi, wr_lo, b_r, *, tm=256):
    n, d = x.shape
    row = lambda i: (i, 0)
    const = lambda i: (0, 0)
    return pl.pallas_call(
        _outproj_router_kernel,
        grid=(n // tm,),
        in_specs=[pl.BlockSpec((tm, GDN_W), row), pl.BlockSpec((tm, HG_W + SC_W), row), pl.BlockSpec((tm, d), row),
                  pl.BlockSpec((d, d), const), pl.BlockSpec((1, d), const),
                  pl.BlockSpec((d, LANES), const), pl.BlockSpec((d, LANES), const), pl.BlockSpec((1, LANES), const)],
        out_specs=[pl.BlockSpec((tm, d), row), pl.BlockSpec((tm, LANES), row), pl.BlockSpec((tm, LANES), row)],
        out_shape=[jax.ShapeDtypeStruct((n, d), F32), jax.ShapeDtypeStruct((n, LANES), jnp.int32),
                   jax.ShapeDtypeStruct((n, LANES), F32)],
        compiler_params=pltpu.CompilerParams(dimension_semantics=("arbitrary",), vmem_limit_bytes=VMEM_LIMIT),
        name="outproj_router",
    )(ya, ybc, x, w_out, norm_w.reshape(1, d), wr_hi, wr_lo, b_r)


def _moe_plan(eid, n_tok):
    blk = MOE_BLK
    n_assign = n_tok * TOP_K
    n_blocks = -(-n_assign // blk) + N_EXP
    flat_e = eid.reshape(-1).astype(jnp.int32)
    shift = max(1, (n_assign - 1).bit_length())
    keys = lax.sort(flat_e * (1 << shift) + jnp.arange(n_assign, dtype=jnp.int32))
    order = keys & ((1 << shift) - 1)
    experts = jnp.arange(N_EXP, dtype=jnp.int32)
    counts = jnp.sum((flat_e[:, None] == experts[None, :]).astype(jnp.int32), axis=0)
    starts = jnp.cumsum(counts) - counts
    padded = (counts + blk - 1) // blk * blk
    pad_end = jnp.cumsum(padded)
    pad_start = pad_end - padded
    n_active = (pad_end[-1] // blk).astype(jnp.int32)
    bidx = jnp.arange(n_blocks, dtype=jnp.int32)
    bstart = bidx * blk
    e_b = jnp.minimum(jnp.sum((bstart[:, None] >= pad_end[None, :]).astype(jnp.int32), axis=1), N_EXP - 1)
    j0 = bstart - pad_start[e_b]
    nvalid = jnp.clip(counts[e_b] - j0, 0, blk)
    i = jnp.arange(blk, dtype=jnp.int32)[None, :]
    valid = i < nvalid[:, None]
    src = jnp.clip((starts[e_b] + j0)[:, None] + i, 0, n_assign - 1)
    a = order[src]
    tok = a // TOP_K
    slot = a % TOP_K
    row_tok = jnp.where(valid, tok, 0).astype(jnp.int32)
    trash = TOP_K * n_tok + (bidx % 2)[:, None] * blk + i
    row_dst = jnp.where(valid, slot * n_tok + tok, trash).astype(jnp.int32)
    block_expert = e_b[jnp.minimum(bidx, jnp.maximum(n_active - 1, 0))]
    return block_expert, n_active.reshape(1), row_tok[:, None, :], row_dst[:, None, :]


def _moe_kernel(bexp_ref, nact_ref, x_hbm, tokc_ref, tokn_ref, dst_ref, nw_ref, wg_ref, wu_ref, wd_ref,
                y_hbm, xbuf, ybuf, wgb, wub, wdb, gsem, ssem):
    blk = MOE_BLK
    b = pl.program_id(0)
    nact = nact_ref[0]
    slot = lax.rem(b, 2)
    other = 1 - slot
    active = b < nact

    def gather_start(tok_ref, s):
        for r in range(blk):
            pltpu.make_async_copy(x_hbm.at[pl.ds(tok_ref[0, r], 1), :], xbuf.at[s, pl.ds(r, 1), :], gsem.at[s]).start()

    def gather_wait(s):
        for r in range(blk):
            pltpu.make_async_copy(x_hbm.at[pl.ds(0, 1), :], xbuf.at[s, pl.ds(r, 1), :], gsem.at[s]).wait()

    def scatter_start(s):
        for r in range(blk):
            pltpu.make_async_copy(ybuf.at[s, pl.ds(r, 1), :], y_hbm.at[pl.ds(dst_ref[0, r], 1), :],
                                  ssem.at[s]).start(priority=r % 2)

    def scatter_wait(s):
        for r in range(blk):
            pltpu.make_async_copy(ybuf.at[s, pl.ds(r, 1), :], y_hbm.at[pl.ds(0, 1), :], ssem.at[s]).wait()

    @pl.when(b == 0)
    def _():
        gather_start(tokc_ref, 0)

    @pl.when(active & (b >= 2))
    def _():
        scatter_wait(slot)

    @pl.when(active & ((b == 0) | (bexp_ref[b] != bexp_ref[jnp.maximum(b - 1, 0)])))
    def _():
        wgb[...] = wg_ref[...].astype(BF16)
        wub[...] = wu_ref[...].astype(BF16)
        wdb[...] = wd_ref[...].astype(BF16)

    @pl.when(active)
    def _():
        gather_start(tokn_ref, other)
        gather_wait(slot)
        xv = xbuf[slot]
        hb = (xv * _rms_scale(xv) * nw_ref[...]).astype(BF16)
        g = _dot(hb, wgb[...])
        u = _dot(hb, wub[...])
        hid = (_silu(g) * u).astype(BF16)
        ybuf[slot] = _dot(hid, wdb[...])
        scatter_start(slot)

    @pl.when(b == nact - 1)
    def _():
        gather_wait(other)
        scatter_wait(slot)

        @pl.when(b >= 1)
        def _():
            scatter_wait(other)


def _moe_ffn(x2, plan, norm_w, w_gate, w_up, w_down, *, layer):
    n, d = x2.shape
    blk = MOE_BLK
    block_expert, n_active, row_tok, row_dst = plan
    n_blocks = row_tok.shape[0]
    wmap = lambda b, bexp, nact: (layer, bexp[b], 0, 0)
    grid_spec = pltpu.PrefetchScalarGridSpec(
        num_scalar_prefetch=2,
        grid=(n_blocks,),
        in_specs=[
            pl.BlockSpec(memory_space=pl.ANY),
            pl.BlockSpec((None, 1, blk), lambda b, bexp, nact: (b, 0, 0), memory_space=pltpu.SMEM),
            pl.BlockSpec((None, 1, blk), lambda b, bexp, nact: (jnp.minimum(b + 1, n_blocks - 1), 0, 0),
                         memory_space=pltpu.SMEM),
            pl.BlockSpec((None, 1, blk), lambda b, bexp, nact: (b, 0, 0), memory_space=pltpu.SMEM),
            pl.BlockSpec((1, d), lambda b, bexp, nact: (0, 0)),
            pl.BlockSpec((None, None, d, D_FF), wmap),
            pl.BlockSpec((None, None, d, D_FF), wmap),
            pl.BlockSpec((None, None, D_FF, d), wmap),
        ],
        out_specs=pl.BlockSpec(memory_space=pl.ANY),
        scratch_shapes=[pltpu.VMEM((2, blk, d), F32), pltpu.VMEM((2, blk, d), F32),
                        pltpu.VMEM((d, D_FF), BF16), pltpu.VMEM((d, D_FF), BF16), pltpu.VMEM((D_FF, d), BF16),
                        pltpu.SemaphoreType.DMA((2,)), pltpu.SemaphoreType.DMA((2,))],
    )
    return pl.pallas_call(
        _moe_kernel,
        grid_spec=grid_spec,
        out_shape=jax.ShapeDtypeStruct((TOP_K * n + 2 * blk, d), F32),
        compiler_params=pltpu.CompilerParams(dimension_semantics=("arbitrary",), vmem_limit_bytes=VMEM_LIMIT),
        name="moe_ffn",
    )(block_expert, n_active, x2, row_tok, row_tok, row_dst, norm_w.reshape(1, d), w_gate, w_up, w_down)


def _split_w_in(w_in_l):
    g0 = 4 * GDN_W
    w_main = jnp.concatenate([w_in_l[:, :g0], w_in_l[:, g0 + N_GATE_COLS:]], axis=1).astype(BF16)
    w_gates = jnp.pad(w_in_l[:, g0:g0 + N_GATE_COLS], ((0, 0), (0, LANES - N_GATE_COLS))).astype(BF16)
    return w_main, w_gates


def _router_weights(w_rg, b_rg, w_re, b_re):
    d = w_rg.shape[0]
    pad = LANES - N_GROUPS - N_EXP
    w_r = jnp.concatenate([w_rg, w_re, jnp.zeros((d, pad), F32)], axis=1).astype(F32)
    b_r = jnp.concatenate([b_rg, b_re, jnp.zeros((pad,), F32)]).astype(F32).reshape(1, LANES)
    hi, lo = _split_hi_lo(w_r)
    return hi, lo, b_r


def kernel(x, attn_norm_w, w_in, gdn_conv_w, gdn_a_log, gdn_dt_bias, gdn_norm_w, hgrn_lower_bounds, hgrn_norm_w, sc_conv_w, w_out, ffn_norm_w, w_router_group, b_router_group, w_router_expert, b_router_expert, w_gate, w_up, w_down, final_norm_w):
    bsz, t, d = x.shape
    n = bsz * t
    depth = w_in.shape[0]
    xs = x.reshape(n, d).astype(F32)
    y2 = wts = None
    for l in range(depth):
        w_main, w_gates = _split_w_in(w_in[l])
        if y2 is None:
            h, gates = _combine_norm(xs, None, None, attn_norm_w[l], w_gates, final=False)
        else:
            xs, h, gates = _combine_norm(xs, y2, wts, attn_norm_w[l], w_gates, final=False)
        proj = _inproj(h, w_main)
        ya = _gdn(proj, gates, gdn_conv_w[l], gdn_a_log[l], gdn_dt_bias[l], gdn_norm_w[l], bsz=bsz, t=t)
        ybc = _hgrn_sc(proj, hgrn_lower_bounds, hgrn_norm_w[l], sc_conv_w[l], bsz=bsz, t=t, layer=l)
        wr_hi, wr_lo, b_r = _router_weights(w_router_group[l], b_router_group[l], w_router_expert[l], b_router_expert[l])
        xs, eid, wts = _outproj_router(ya, ybc, xs, w_out[l].astype(BF16), ffn_norm_w[l], wr_hi, wr_lo, b_r)
        plan = _moe_plan(eid[:, :TOP_K], n)
        y2 = _moe_ffn(xs, plan, ffn_norm_w[l], w_gate, w_up, w_down, layer=l)
    (out,) = _combine_norm(xs, y2, wts, final_norm_w, None, final=True)
    return out.reshape(bsz, t, d).astype(x.dtype)
```
